```python
import math
import jax
import jax.numpy as jnp
from jax import lax
import numpy as np

D_MODEL = 1024
BATCH = 1
SEQ = 16384
DEPTH = 2

HEAD_DIM = 64
GRID_W = 64
Q_BLOCK = 128
EPS = 1e-6
NEG_INF = -1e30
ROPE_THETA = 10000.0

A_HEADS = 4
A_KV_HEADS = 2
A_HALF_WINDOW = 128
B_PATTERNS = ((128, 1), (512, 4), (2048, 16))
B_GROUPS = 3
B_HEADS = 4
C_HEADS = 4
C_Q_RANK = 256
C_KV_RANK = 128
C_NOPE = 64
C_ROPE = 32
C_QK_DIM = C_NOPE + C_ROPE
C_V = 64
D_HEADS = 4
D_KV_HEADS = 2

N_BRANCHES = 4
BRANCH_WIDTH = 4 * HEAD_DIM

REL_BUCKETS = 32
REL_MAX_DIST = 1024
REL_HEADS = A_HEADS + B_GROUPS * B_HEADS

PEER_HEADS = 8
PEER_N_KEYS = 128
PEER_N_EXPERTS = PEER_N_KEYS * PEER_N_KEYS
PEER_KEY_DIM = 128
PEER_TOPK = 16
PEER_TOKEN_BLOCK = 128

IN_SPLITS = (
    A_HEADS * HEAD_DIM, A_KV_HEADS * HEAD_DIM, A_KV_HEADS * HEAD_DIM,
    B_GROUPS * B_HEADS * HEAD_DIM, B_GROUPS * B_HEADS * HEAD_DIM, B_GROUPS * B_HEADS * HEAD_DIM,
    C_Q_RANK, C_KV_RANK, C_ROPE,
    D_HEADS * HEAD_DIM, D_KV_HEADS * HEAD_DIM, D_KV_HEADS * HEAD_DIM,
    N_BRANCHES * D_MODEL,
)
IN_WIDTH = sum(IN_SPLITS)

kernel_name = 'hybrid_gated_four_mixer_peer_encoder'


def rms_norm(x, gain):
    x32 = x.astype(jnp.float32)
    y = x32 * lax.rsqrt(jnp.mean(x32 * x32, axis=-1, keepdims=True) + EPS)
    return (y * gain.astype(jnp.float32)).astype(x.dtype)


def rope(x, pos):
    half = x.shape[-1] // 2
    freqs = ROPE_THETA ** (-jnp.arange(half, dtype=jnp.float32) / half)
    ang = pos.astype(jnp.float32)[:, None] * freqs[None, :]
    cos = jnp.cos(ang)[:, None, :]
    sin = jnp.sin(ang)[:, None, :]
    x32 = x.astype(jnp.float32)
    x1, x2 = x32[..., :half], x32[..., half:]
    return jnp.concatenate([x1 * cos - x2 * sin, x1 * sin + x2 * cos], axis=-1).astype(x.dtype)


def axial_rope(x, row, col):
    half = x.shape[-1] // 2
    return jnp.concatenate([rope(x[..., :half], row), rope(x[..., half:], col)], axis=-1)


def rel_bucket(rel):
    nb = REL_BUCKETS // 2
    max_exact = nb // 2
    ret = jnp.where(rel > 0, nb, 0)
    n = jnp.abs(rel)
    nf = jnp.maximum(n, 1).astype(jnp.float32)
    large = max_exact + (jnp.log(nf / max_exact) / math.log(REL_MAX_DIST / max_exact)
                         * (nb - max_exact)).astype(jnp.int32)
    large = jnp.minimum(large, nb - 1)
    return ret + jnp.where(n < max_exact, n, large)


def banded_attention(q, k, v, half_w, dilation, rel_table, sink):
    f32 = jnp.float32
    N, L, H, Dh = q.shape
    Hk = k.shape[2]
    G = H // Hk
    blk = min(Q_BLOCK, L)
    nb = -(-L // blk)
    Lp = nb * blk
    span = blk + 2 * half_w
    qp = jnp.pad(q.astype(f32), ((0, 0), (0, Lp - L), (0, 0), (0, 0)))
    kv_pad = ((0, 0), (half_w, Lp - L + half_w), (0, 0), (0, 0))
    kp = jnp.pad(k.astype(f32), kv_pad)
    vp = jnp.pad(v.astype(f32), kv_pad)
    kidx = jnp.arange(nb)[:, None] * blk + jnp.arange(span)[None, :]
    kb = kp[:, kidx]
    vb = vp[:, kidx]
    qb = qp.reshape(N, nb, blk, Hk, G, Dh)
    s = jnp.einsum('nbqkgd,nbskd->nbkgqs', qb, kb) * (Dh ** -0.5)
    rel = jnp.arange(span)[None, :] - half_w - jnp.arange(blk)[:, None]
    bias = rel_table.astype(f32)[rel_bucket(rel * dilation)]
    bias = bias.transpose(2, 0, 1).reshape(Hk, G, blk, span)
    kpos = kidx - half_w
    valid = (jnp.abs(rel) <= half_w)[None] & ((kpos >= 0) & (kpos < L))[:, None, :]
    s = jnp.where(valid[None, :, None, None], s + bias, NEG_INF)
    m = jnp.max(s, axis=-1, keepdims=True)
    if sink is not None:
        sk = sink.astype(f32).reshape(Hk, G, 1, 1)
        m = jnp.maximum(m, sk)
    p = jnp.exp(s - m)
    denom = jnp.sum(p, axis=-1, keepdims=True)
    if sink is not None:
        denom = denom + jnp.exp(sk - m)
    out = jnp.einsum('nbkgqs,nbskd->nbqkgd', p / denom, vb).reshape(N, Lp, H, Dh)[:, :L]
    lse = (m + jnp.log(denom))[..., 0].transpose(0, 1, 4, 2, 3).reshape(N, Lp, H)[:, :L]
    return out.astype(v.dtype), lse


def dense_attention(q, k, v):
    f32 = jnp.float32
    Bn, S, H, Dq = q.shape
    Hk, Dv = k.shape[2], v.shape[3]
    G = H // Hk
    nb = S // Q_BLOCK
    qb = q.astype(f32).reshape(Bn, nb, Q_BLOCK, Hk, G, Dq).swapaxes(0, 1)
    k32, v32 = k.astype(f32), v.astype(f32)
    scale = Dq ** -0.5

    def block(qblk):
        s = jnp.einsum('bqkgd,bskd->bkgqs', qblk, k32) * scale
        p = jax.nn.softmax(s, axis=-1)
        return jnp.einsum('bkgqs,bskd->bqkgd', p, v32)

    out = lax.map(block, qb)
    return out.swapaxes(0, 1).reshape(Bn, S, H, Dv).astype(v.dtype)


def to_strided(t, dil):
    Bn, S = t.shape[:2]
    return t.reshape(Bn, S // dil, dil, *t.shape[2:]).swapaxes(1, 2).reshape(Bn * dil, S // dil, *t.shape[2:])


def from_strided(t, dil, Bn):
    L = t.shape[1]
    return t.reshape(Bn, dil, L, *t.shape[2:]).swapaxes(1, 2).reshape(Bn, dil * L, *t.shape[2:])


def mixer_a(q, k, v, q_gain, k_gain, rel_table, sink):
    q = rms_norm(q, q_gain)
    k = rms_norm(k, k_gain)
    out, _ = banded_attention(q, k, v, A_HALF_WINDOW, 1, rel_table, sink)
    return out


def mixer_b(q, k, v, q_gain, k_gain, rel_table):
    Bn = q.shape[0]
    outs, lses = [], []
    for g, (window, dil) in enumerate(B_PATTERNS):
        half = window // (2 * dil)
        qg = rms_norm(q[:, :, g], q_gain[g])
        kg = rms_norm(k[:, :, g], k_gain[g])
        o, lse = banded_attention(to_strided(qg, dil), to_strided(kg, dil), to_strided(v[:, :, g], dil),
                                  half, dil, rel_table[:, g * B_HEADS:(g + 1) * B_HEADS], None)
        outs.append(from_strided(o, dil, Bn).astype(jnp.float32))
        lses.append(from_strided(lse, dil, Bn))
    w = jax.nn.softmax(jnp.stack(lses, axis=0), axis=0)
    return jnp.einsum('gbsh,gbshd->bshd', w, jnp.stack(outs, axis=0)).astype(v.dtype)


def mixer_c(cq, ckv, kpe, cq_lat_gain, ckv_lat_gain, w_uq, w_ukv, q_gain, k_gain, pos):
    Bn, S = cq.shape[:2]
    q = (rms_norm(cq, cq_lat_gain) @ w_uq).reshape(Bn, S, C_HEADS, C_QK_DIM)
    kv = (rms_norm(ckv, ckv_lat_gain) @ w_ukv).reshape(Bn, S, C_HEADS, C_NOPE + C_V)
    k_nope, v = kv[..., :C_NOPE], kv[..., C_NOPE:]
    q_nope = rms_norm(q[..., :C_NOPE], q_gain[:C_NOPE])
    q_pe = rope(rms_norm(q[..., C_NOPE:], q_gain[C_NOPE:]), pos)
    k_nope = rms_norm(k_nope, k_gain[:C_NOPE])
    k_pe = rope(rms_norm(kpe[:, :, None, :], k_gain[C_NOPE:]), pos)
    qf = jnp.concatenate([q_nope, q_pe], axis=-1)
    kf = jnp.concatenate([k_nope, jnp.broadcast_to(k_pe, (Bn, S, C_HEADS, C_ROPE))], axis=-1)
    return dense_attention(qf, kf, v)


def mixer_d(q, k, v, q_gain, k_gain, row, col):
    q = axial_rope(rms_norm(q, q_gain), row, col)
    k = axial_rope(rms_norm(k, k_gain), row, col)
    return dense_attention(q, k, v)


def peer_ffn(x, w_q, sub_keys, w_u, w_v):
    Bn, S, D = x.shape
    T = Bn * S
    xt = x.reshape(T, D)
    half = PEER_KEY_DIM // 2
    kk = PEER_TOPK * PEER_TOPK
    q = (xt @ w_q).reshape(T, PEER_HEADS, 2, half).astype(jnp.float32)
    scores = jnp.einsum('thcd,hcnd->thcn', q, sub_keys.astype(jnp.float32))
    s_half, i_half = lax.top_k(scores, PEER_TOPK)
    cand_s = (s_half[:, :, 0, :, None] + s_half[:, :, 1, None, :]).reshape(T, PEER_HEADS, kk)
    cand_i = (i_half[:, :, 0, :, None] * PEER_N_KEYS + i_half[:, :, 1, None, :]).reshape(T, PEER_HEADS, kk)
    top_s, top_pos = lax.top_k(cand_s, PEER_TOPK)
    idx = jnp.take_along_axis(cand_i, top_pos, axis=-1)
    gate = jax.nn.softmax(top_s, axis=-1)
    nb = T // PEER_TOKEN_BLOCK
    E = PEER_HEADS * PEER_TOPK

    def block(args):
        xb, ib, gb = args
        act = jax.nn.gelu(jnp.einsum('td,ted->te', xb, w_u[ib]).astype(jnp.float32), approximate=False)
        return jnp.einsum('te,ted->td', (act * gb).astype(xb.dtype), w_v[ib])

    out = lax.map(block, (xt.reshape(nb, PEER_TOKEN_BLOCK, D),
                          idx.reshape(nb, PEER_TOKEN_BLOCK, E),
                          gate.reshape(nb, PEER_TOKEN_BLOCK, E)))
    return out.reshape(Bn, S, D)


def setup_inputs(seed: int = 0) -> dict:
    key = jax.random.key(seed)
    ks = jax.random.split(key, 26)
    f32 = jnp.float32

    def nrm(k, shape, scale):
        return jax.random.normal(k, shape, f32) * scale

    def gain(k, shape):
        return 1.0 + 0.02 * jax.random.normal(k, shape, f32)

    L = DEPTH
    return {
        'x': nrm(ks[0], (BATCH, SEQ, D_MODEL), 1.0),
        'rel_bias': nrm(ks[1], (REL_BUCKETS, REL_HEADS), 0.5),
        'norm1_gain': gain(ks[2], (L, D_MODEL)),
        'w_in': nrm(ks[3], (L, D_MODEL, IN_WIDTH), D_MODEL ** -0.5),
        'gate_bias': nrm(ks[4], (L, N_BRANCHES * D_MODEL), 0.01),
        'a_q_gain': gain(ks[5], (L, HEAD_DIM)),
        'a_k_gain': gain(ks[6], (L, HEAD_DIM)),
        'a_sink': nrm(ks[7], (L, A_HEADS), 0.5),
        'b_q_gain': gain(ks[8], (L, B_GROUPS, HEAD_DIM)),
        'b_k_gain': gain(ks[9], (L, B_GROUPS, HEAD_DIM)),
        'c_q_lat_gain': gain(ks[10], (L, C_Q_RANK)),
        'c_kv_lat_gain': gain(ks[11], (L, C_KV_RANK)),
        'c_w_uq': nrm(ks[12], (L, C_Q_RANK, C_HEADS * C_QK_DIM), C_Q_RANK ** -0.5),
        'c_w_ukv': nrm(ks[13], (L, C_KV_RANK, C_HEADS * (C_NOPE + C_V)), C_KV_RANK ** -0.5),
        'c_q_gain': gain(ks[14], (L, C_QK_DIM)),
        'c_k_gain': gain(ks[15], (L, C_QK_DIM)),
        'd_q_gain': gain(ks[16], (L, HEAD_DIM)),
        'd_k_gain': gain(ks[17], (L, HEAD_DIM)),
        'w_branch': nrm(ks[18], (L, N_BRANCHES, BRANCH_WIDTH, D_MODEL), BRANCH_WIDTH ** -0.5),
        'w_out': nrm(ks[19], (L, D_MODEL, D_MODEL), D_MODEL ** -0.5),
        'norm2_gain': gain(ks[20], (L, D_MODEL)),
        'peer_w_q': nrm(ks[21], (L, D_MODEL, PEER_HEADS * PEER_KEY_DIM), D_MODEL ** -0.5),
        'peer_sub_keys': nrm(ks[22], (L, PEER_HEADS, 2, PEER_N_KEYS, PEER_KEY_DIM // 2), (PEER_KEY_DIM // 2) ** -0.5),
        'peer_w_u': nrm(ks[23], (L, PEER_N_EXPERTS, D_MODEL), D_MODEL ** -0.5),
        'peer_w_v': nrm(ks[24], (L, PEER_N_EXPERTS, D_MODEL), PEER_HEADS ** -0.5),
    }


def reference(x, rel_bias, norm1_gain, w_in, gate_bias, a_q_gain, a_k_gain, a_sink, b_q_gain, b_k_gain,
              c_q_lat_gain, c_kv_lat_gain, c_w_uq, c_w_ukv, c_q_gain, c_k_gain, d_q_gain, d_k_gain,
              w_branch, w_out, norm2_gain, peer_w_q, peer_sub_keys, peer_w_u, peer_w_v):
    Bn, S, D = x.shape
    rows = S // GRID_W
    pos = jnp.arange(S, dtype=jnp.int32)
    row = jnp.broadcast_to(jnp.arange(rows, dtype=jnp.int32)[:, None], (rows, GRID_W)).reshape(S)
    col = jnp.broadcast_to(jnp.arange(GRID_W, dtype=jnp.int32)[None, :], (rows, GRID_W)).reshape(S)
    offsets = [int(o) for o in np.cumsum(IN_SPLITS)[:-1]]

    def heads(t, n):
        return t.reshape(Bn, S, n, HEAD_DIM)

    def groups(t):
        return t.reshape(Bn, S, B_GROUPS, B_HEADS, HEAD_DIM)

    h = x
    for l in range(DEPTH):
        hn = rms_norm(h, norm1_gain[l])
        proj = hn @ w_in[l]
        (aq, ak, av, bq, bk, bv, cq, ckv, ckr, dq, dk, dv, gpre) = jnp.split(proj, offsets, axis=-1)
        ya = mixer_a(heads(aq, A_HEADS), heads(ak, A_KV_HEADS), heads(av, A_KV_HEADS),
                     a_q_gain[l], a_k_gain[l], rel_bias[:, :A_HEADS], a_sink[l])
        yb = mixer_b(groups(bq), groups(bk), groups(bv), b_q_gain[l], b_k_gain[l], rel_bias[:, A_HEADS:])
        yc = mixer_c(cq, ckv, ckr, c_q_lat_gain[l], c_kv_lat_gain[l], c_w_uq[l], c_w_ukv[l],
                     c_q_gain[l], c_k_gain[l], pos)
        yd = mixer_d(heads(dq, D_HEADS), heads(dk, D_KV_HEADS), heads(dv, D_KV_HEADS),
                     d_q_gain[l], d_k_gain[l], row, col)
        ys = jnp.stack([ya.reshape(Bn, S, BRANCH_WIDTH), yb.reshape(Bn, S, BRANCH_WIDTH),
                        yc.reshape(Bn, S, BRANCH_WIDTH), yd.reshape(Bn, S, BRANCH_WIDTH)], axis=2)
        branches = jnp.einsum('bsnc,ncd->bsnd', ys, w_branch[l])
        gates = jax.nn.sigmoid(gpre.reshape(Bn, S, N_BRANCHES, D) + gate_bias[l].reshape(N_BRANCHES, D))
        merged = jnp.einsum('bsnd,bsnd->bsd', gates, branches)
        h = h + merged @ w_out[l]
        h = h + peer_ffn(rms_norm(h, norm2_gain[l]), peer_w_q[l], peer_sub_keys[l], peer_w_u[l], peer_w_v[l])
    return h
```

```python
import functools
import math

import jax
import jax.numpy as jnp
import numpy as np
from jax import lax
from jax.experimental import pallas as pl
from jax.experimental.pallas import tpu as pltpu

F32 = jnp.float32
BF16 = jnp.bfloat16

D_MODEL = 1024
HEAD_DIM = 64
GRID_W = 64
EPS = 1e-6
NEG_INF = -1e30
ROPE_THETA = 10000.0

A_HEADS, A_KV_HEADS, A_HALF_WINDOW = 4, 2, 128
B_PATTERNS = ((128, 1), (512, 4), (2048, 16))
B_GROUPS, B_HEADS = 3, 4
C_HEADS, C_Q_RANK, C_KV_RANK, C_NOPE, C_ROPE, C_V = 4, 256, 128, 64, 32, 64
C_QK_DIM = C_NOPE + C_ROPE
D_HEADS, D_KV_HEADS = 4, 2
N_BRANCHES = 4
REL_BUCKETS, REL_MAX_DIST = 32, 1024
PEER_HEADS, PEER_N_KEYS, PEER_KEY_DIM, PEER_TOPK = 8, 128, 128, 16
PEER_HALF = PEER_KEY_DIM // 2

IN_SPLITS = (256, 128, 128, 768, 768, 768, C_Q_RANK, C_KV_RANK, C_ROPE, 256, 128, 128, N_BRANCHES * D_MODEL)
IN_OFFS = tuple(int(o) for o in np.cumsum((0,) + IN_SPLITS))

LANES = 128
QBLK = 128
VMEM_LIMIT = 56 * 1024 * 1024

P_AQ, P_AK, P_AV = 0, 256, 512
P_BQ, P_BK, P_BV = 768, 1536, 2304
P_CQ, P_CKV, P_CKR = 3072, 3328, 3456
P_DQ, P_DK, P_DV = 3968, 4480, 4736
P_WIDTH = 4992

TM_PROJ = 256
TM_MERGE = 256
TQ_FLASH = 512
TM_PREP = 256
TM_PEER = 256
ET_PEER = 1024


def _cparams(sem):
    return pltpu.CompilerParams(dimension_semantics=sem, vmem_limit_bytes=VMEM_LIMIT)


def _full(shape):
    n = len(shape)
    return pl.BlockSpec(shape, lambda *_: (0,) * n)


def _seg_norm(t, bd, gain):
    ms = jnp.dot((t * t).astype(BF16), bd, preferred_element_type=F32)
    return t * lax.rsqrt(ms + EPS) * gain


def _rope128(x, tab_ref):
    up = pltpu.roll(x, LANES - 16, axis=1)
    dn = pltpu.roll(x, 16, axis=1)
    return x * tab_ref[0] + up * tab_ref[1] + dn * tab_ref[2]


def _proj_kernel(h_ref, g1_ref, w_ref, gains_ref, bd64_ref, bd128_ref, bdc_ref, bd256_ref, bdkv_ref,
                 wuq_ref, wukv_ref, gcq_ref, gckn_ref, tabc_ref, tabd_ref,
                 qa_ref, ka_ref, va_ref, qb_ref, kb_ref, vb_ref,
                 qc_ref, kct_ref, vc_ref, qd_ref, kdt_ref, vd_ref):
    x = h_ref[...]
    ms = jnp.mean(x * x, axis=-1, keepdims=True)
    hn = (x * lax.rsqrt(ms + EPS) * g1_ref[...]).astype(BF16)

    def proj(off, width):
        return jnp.dot(hn, w_ref[:, off:off + width], preferred_element_type=F32)

    def gain(off, width):
        return gains_ref[:, off:off + width]

    bd64 = bd64_ref[...]
    bd128 = bd128_ref[...]
    bdc = bdc_ref[...]

    qa_ref[...] = _seg_norm(proj(P_AQ, 256), bd64, gain(P_AQ, 256)).astype(BF16)
    ka_ref[...] = _seg_norm(proj(P_AK, 256), bd64, gain(P_AK, 256)).astype(BF16)
    va_ref[...] = proj(P_AV, 256).astype(BF16)

    for g in range(B_GROUPS):
        o = g * 256
        qb_ref[:, o:o + 256] = _seg_norm(proj(P_BQ + o, 256), bd64, gain(P_BQ + o, 256)).astype(BF16)
        kb_ref[:, o:o + 256] = _seg_norm(proj(P_BK + o, 256), bd64, gain(P_BK + o, 256)).astype(BF16)
        vb_ref[:, o:o + 256] = proj(P_BV + o, 256).astype(BF16)

    cq = _seg_norm(proj(P_CQ, 256), bd256_ref[...], gain(P_CQ, 256)).astype(BF16)
    ckv = _seg_norm(proj(P_CKV, 128), bdkv_ref[...], gain(P_CKV, 128)).astype(BF16)
    for hb in range(2):
        o = hb * 256
        q = jnp.dot(cq, wuq_ref[:, o:o + 256], preferred_element_type=F32)
        q = _seg_norm(q, bdc, gcq_ref[:, o:o + 256])
        kn = jnp.dot(ckv, wukv_ref[:, o:o + 256], preferred_element_type=F32)
        kn = _seg_norm(kn, bd128, gckn_ref[:, o:o + 256])
        kp = _seg_norm(proj(P_CKR + o, 256), bdc, gain(P_CKR + o, 256))
        vc_ref[:, o:o + 256] = jnp.dot(ckv, wukv_ref[:, 512 + o:512 + o + 256],
                                       preferred_element_type=F32).astype(BF16)
        for j in range(2):
            c = j * LANES
            hd = hb * 2 + j
            qc_ref[:, o + c:o + c + LANES] = _rope128(q[:, c:c + LANES], tabc_ref).astype(BF16)
            k = kn[:, c:c + LANES] + _rope128(kp[:, c:c + LANES], tabc_ref)
            kct_ref[hd, 0] = k.T.astype(BF16)

    for hb in range(2):
        o = hb * 256
        q = _seg_norm(proj(P_DQ + o, 256), bd128, gain(P_DQ + o, 256))
        for j in range(2):
            c = j * LANES
            qd_ref[:, o + c:o + c + LANES] = _rope128(q[:, c:c + LANES], tabd_ref).astype(BF16)
    k = _seg_norm(proj(P_DK, 256), bd128, gain(P_DK, 256))
    for j in range(D_KV_HEADS):
        c = j * LANES
        kdt_ref[j, 0] = _rope128(k[:, c:c + LANES], tabd_ref).T.astype(BF16)
    vd_ref[...] = proj(P_DV, 256).astype(BF16)


def _proj_call(h, lw, consts, tabc, tabd):
    S = h.shape[0]
    tm = TM_PROJ
    nt = S // tm
    row = lambda w: pl.BlockSpec((tm, w), lambda i: (i, 0))
    in_specs = [
        row(D_MODEL), _full((1, D_MODEL)), _full((D_MODEL, P_WIDTH)), _full((1, P_WIDTH)),
        _full((256, 256)), _full((256, 256)), _full((256, 256)), _full((256, 256)), _full((128, 128)),
        _full((C_Q_RANK, 512)), _full((C_KV_RANK, 1024)), _full((1, 512)), _full((1, 512)),
        pl.BlockSpec((3, tm, LANES), lambda i: (0, i, 0)),
        pl.BlockSpec((3, tm, LANES), lambda i: (0, i, 0)),
    ]
    out_shape = [
        jax.ShapeDtypeStruct((S, 256), BF16), jax.ShapeDtypeStruct((S, 256), BF16),
        jax.ShapeDtypeStruct((S, 256), BF16),
        jax.ShapeDtypeStruct((S, 768), BF16), jax.ShapeDtypeStruct((S, 768), BF16),
        jax.ShapeDtypeStruct((S, 768), BF16),
        jax.ShapeDtypeStruct((S, 512), BF16), jax.ShapeDtypeStruct((C_HEADS, nt, LANES, tm), BF16),
        jax.ShapeDtypeStruct((S, 512), BF16),
        jax.ShapeDtypeStruct((S, 512), BF16), jax.ShapeDtypeStruct((D_KV_HEADS, nt, LANES, tm), BF16),
        jax.ShapeDtypeStruct((S, 256), BF16),
    ]
    out_specs = [
        row(256), row(256), row(256), row(768), row(768), row(768),
        row(512), pl.BlockSpec((C_HEADS, 1, LANES, tm), lambda i: (0, i, 0, 0)), row(512),
        row(512), pl.BlockSpec((D_KV_HEADS, 1, LANES, tm), lambda i: (0, i, 0, 0)), row(256),
    ]
    return pl.pallas_call(
        _proj_kernel, grid=(nt,), in_specs=in_specs, out_specs=out_specs, out_shape=out_shape,
        compiler_params=_cparams(("parallel",)), name="proj",
    )(h, lw["g1"], lw["w1"], lw["gains"], consts["bd64"], consts["bd128"], consts["bdc"],
      consts["bd256"], consts["bdkv"], lw["wuq"], lw["wukv"], lw["gcq"], lw["gckn"], tabc, tabd)


def _banded_kernel(*refs, seq_len, has_sink):
    if has_sink:
        sink_ref, refs = refs[0], refs[1:]
    q_ref, kp_ref, kc_ref, kn_ref, vp_ref, vc_ref, vn_ref, bias_ref, o_ref, lse_ref = refs
    i = pl.program_id(1)
    q = q_ref[...]
    k = jnp.concatenate([kp_ref[...], kc_ref[...], kn_ref[...]], axis=0)
    v = jnp.concatenate([vp_ref[...], vc_ref[...], vn_ref[...]], axis=0)
    kpos = (i - 1) * QBLK + lax.broadcasted_iota(jnp.int32, (1, 3 * QBLK), 1)
    valid = (kpos >= 0) & (kpos < seq_len)
    lane = lax.broadcasted_iota(jnp.int32, (1, 4 * HEAD_DIM), 1)
    out = jnp.zeros((QBLK, 4 * HEAD_DIM), F32)
    lse = jnp.zeros((QBLK, 4 * HEAD_DIM), F32)
    for h in range(4):
        hmask = (lane >= h * HEAD_DIM) & (lane < (h + 1) * HEAD_DIM)
        qh = jnp.where(hmask, q, jnp.zeros_like(q))
        s = lax.dot_general(qh, k, (((1,), (1,)), ((), ())), preferred_element_type=F32)
        s = jnp.where(valid, s + bias_ref[h], NEG_INF)
        m = jnp.max(s, axis=-1, keepdims=True)
        if has_sink:
            m = jnp.maximum(m, sink_ref[h])
        p = jnp.exp(s - m)
        denom = jnp.sum(p, axis=-1, keepdims=True)
        if has_sink:
            denom = denom + jnp.exp(sink_ref[h] - m)
        pv = jnp.dot(p.astype(BF16), v, preferred_element_type=F32)
        out = out + jnp.where(hmask, pv / denom, 0.0)
        lse = lse + jnp.where(hmask, m + jnp.log(denom), 0.0)
    o_ref[...] = out.astype(o_ref.dtype)
    lse_ref[...] = lse


def _banded_call(q, k, v, bias, sink, dil):
    L = q.shape[0]
    nb = L // QBLK
    blk = lambda f: pl.BlockSpec((QBLK, 256), f)
    cur = lambda r, i: (i, r)
    prev = lambda r, i: (jnp.maximum(i - 1, 0), r)
    nxt = lambda r, i: (jnp.minimum(i + 1, nb - 1), r)
    in_specs = [blk(cur), blk(prev), blk(cur), blk(nxt), blk(prev), blk(cur), blk(nxt),
                _full((4, QBLK, 3 * QBLK))]
    args = [q, k, k, k, v, v, v, bias]
    if sink is not None:
        in_specs = [pl.BlockSpec(memory_space=pltpu.SMEM)] + in_specs
        args = [sink] + args
    return pl.pallas_call(
        functools.partial(_banded_kernel, seq_len=L, has_sink=sink is not None),
        grid=(dil, nb), in_specs=in_specs,
        out_specs=[blk(cur), blk(cur)],
        out_shape=[jax.ShapeDtypeStruct(q.shape, BF16), jax.ShapeDtypeStruct(q.shape, F32)],
        compiler_params=_cparams(("parallel", "parallel")), name="banded",
    )(*args)


def _flash_kernel(q_ref, kt_ref, v_ref, o_ref, *, n_chunks, chunk):
    q = q_ref[...]
    tq = q.shape[0]

    def body(c, carry):
        m, l, acc = carry
        s = jnp.dot(q, kt_ref[c], preferred_element_type=F32)
        m_new = jnp.maximum(m, jnp.max(s, axis=-1, keepdims=True))
        alpha = jnp.exp(m - m_new)
        p = jnp.exp(s - m_new)
        l = alpha * l + jnp.sum(p, axis=-1, keepdims=True)
        vv = v_ref[pl.ds(pl.multiple_of(c * chunk, chunk), chunk), :]
        acc = alpha * acc + jnp.dot(p.astype(BF16), vv, preferred_element_type=F32)
        return m_new, l, acc

    init = (jnp.full((tq, 1), -jnp.inf, F32), jnp.zeros((tq, 1), F32), jnp.zeros((tq, LANES), F32))
    _, l, acc = lax.fori_loop(0, n_chunks, body, init)
    o_ref[...] = (acc / l).astype(o_ref.dtype)


def _flash_call(q, kt, v, n_heads, group):
    S = q.shape[0]
    n_chunks, chunk = kt.shape[1], kt.shape[3]
    tq = min(TQ_FLASH, S)
    return pl.pallas_call(
        functools.partial(_flash_kernel, n_chunks=n_chunks, chunk=chunk),
        grid=(n_heads, S // tq),
        in_specs=[pl.BlockSpec((tq, LANES), lambda h, i: (i, h)),
                  pl.BlockSpec((None, n_chunks, LANES, chunk), lambda h, i: (h // group, 0, 0, 0)),
                  pl.BlockSpec((S, LANES), lambda h, i: (0, h // group))],
        out_specs=pl.BlockSpec((tq, LANES), lambda h, i: (i, h)),
        out_shape=jax.ShapeDtypeStruct((S, n_heads * LANES), BF16),
        compiler_params=_cparams(("parallel", "parallel")), name="flash",
    )(q, kt, v)


def _merge_kernel(h_ref, g1_ref, wg_ref, gb_ref, ya_ref, ob1_ref, ob2_ref, ob3_ref,
                  ls1_ref, ls2_ref, ls3_ref, yc_ref, yd_ref,
                  wba_ref, wbb_ref, wbc_ref, wbd_ref, wo_ref, o_ref):
    x = h_ref[...]
    ms = jnp.mean(x * x, axis=-1, keepdims=True)
    hn = (x * lax.rsqrt(ms + EPS) * g1_ref[...]).astype(BF16)

    l1, l2, l3 = ls1_ref[...], ls2_ref[...], ls3_ref[...]
    mx = jnp.maximum(jnp.maximum(l1, l2), l3)
    e1, e2, e3 = jnp.exp(l1 - mx), jnp.exp(l2 - mx), jnp.exp(l3 - mx)
    yb = (e1 * ob1_ref[...].astype(F32) + e2 * ob2_ref[...].astype(F32)
          + e3 * ob3_ref[...].astype(F32)) / (e1 + e2 + e3)

    ys = (ya_ref[...], yb.astype(BF16), yc_ref[...], yd_ref[...])
    wbs = (wba_ref, wbb_ref, wbc_ref, wbd_ref)
    merged = jnp.zeros((x.shape[0], D_MODEL), F32)
    for n in range(N_BRANCHES):
        o = n * D_MODEL
        gpre = jnp.dot(hn, wg_ref[:, o:o + D_MODEL], preferred_element_type=F32) + gb_ref[:, o:o + D_MODEL]
        br = jnp.dot(ys[n], wbs[n][...], preferred_element_type=F32)
        merged = merged + jax.nn.sigmoid(gpre) * br
    o_ref[...] = x + jnp.dot(merged.astype(BF16), wo_ref[...], preferred_element_type=F32)


def _merge_call(h, lw, ya, obs, lses, yc, yd):
    S = h.shape[0]
    tm = TM_MERGE
    row = lambda w: pl.BlockSpec((tm, w), lambda i: (i, 0))
    in_specs = [row(D_MODEL), _full((1, D_MODEL)), _full((D_MODEL, 4 * D_MODEL)), _full((1, 4 * D_MODEL)),
                row(256), row(256), row(256), row(256), row(256), row(256), row(256), row(512), row(512),
                _full((256, D_MODEL)), _full((256, D_MODEL)), _full((512, D_MODEL)), _full((512, D_MODEL)),
                _full((D_MODEL, D_MODEL))]
    return pl.pallas_call(
        _merge_kernel, grid=(S // tm,), in_specs=in_specs, out_specs=row(D_MODEL),
        out_shape=jax.ShapeDtypeStruct((S, D_MODEL), F32),
        compiler_params=_cparams(("parallel",)), name="merge",
    )(h, lw["g1"], lw["wg"], lw["gb"], ya, obs[0], obs[1], obs[2], lses[0], lses[1], lses[2], yc, yd,
      lw["wba"], lw["wbb"], lw["wbc"], lw["wbd"], lw["wo"])


def _top16_desc(s):
    vals = []
    cur = s
    for _ in range(PEER_TOPK):
        m = jnp.max(cur, axis=0, keepdims=True)
        vals.append(m)
        cur = jnp.where(cur == m, -jnp.inf, cur)
    return vals


def _stack_rows(rows):
    n = len(rows)
    ridx = lax.broadcasted_iota(jnp.int32, (n, rows[0].shape[1]), 0)
    out = jnp.zeros((n, rows[0].shape[1]), F32)
    for r, v in enumerate(rows):
        out = jnp.where(ridx == r, v, out)
    return out


def _peerprep_kernel(h_ref, g2_ref, wqt_ref, keys_ref, xt_ref, th_ref, e1_ref, s2_ref, e2_ref):
    x = h_ref[...]
    ms = jnp.mean(x * x, axis=-1, keepdims=True)
    xn = x * lax.rsqrt(ms + EPS) * g2_ref[...]
    xt = xn.T.astype(BF16)
    xt_ref[...] = xt
    qt = jnp.dot(wqt_ref[...], xt, preferred_element_type=F32).astype(BF16)
    for h in range(PEER_HEADS):
        o = h * PEER_KEY_DIM
        s1 = jnp.dot(keys_ref[h, 0], qt[o:o + PEER_HALF], preferred_element_type=F32)
        s2 = jnp.dot(keys_ref[h, 1], qt[o + PEER_HALF:o + PEER_KEY_DIM], preferred_element_type=F32)
        v1 = _top16_desc(s1)
        v2 = _stack_rows(_top16_desc(s2))
        cand = jnp.concatenate([v1[a] + v2 for a in range(PEER_TOPK)], axis=0)
        top = _top16_desc(cand)
        t0, tau = top[0], top[PEER_TOPK - 1]
        z = top[0] - t0
        z = jnp.exp(z)
        for t in top[1:]:
            z = z + jnp.exp(t - t0)
        v20 = v2[0:1]
        th_ref[h] = tau - s1
        e1_ref[h] = jnp.exp(s1 - v1[0]) / z
        s2_ref[h] = s2
        e2_ref[h] = jnp.exp(s2 - v20)


def _peerprep_call(h, lw):
    S = h.shape[0]
    tm = min(TM_PREP, S)
    hs = pl.BlockSpec((PEER_HEADS, PEER_N_KEYS, tm), lambda i: (0, 0, i))
    hshape = jax.ShapeDtypeStruct((PEER_HEADS, PEER_N_KEYS, S), F32)
    return pl.pallas_call(
        _peerprep_kernel, grid=(S // tm,),
        in_specs=[pl.BlockSpec((tm, D_MODEL), lambda i: (i, 0)), _full((1, D_MODEL)),
                  _full((D_MODEL, D_MODEL)), _full((PEER_HEADS, 2, PEER_N_KEYS, PEER_HALF))],
        out_specs=[pl.BlockSpec((D_MODEL, tm), lambda i: (0, i)), hs, hs, hs, hs],
        out_shape=[jax.ShapeDtypeStruct((D_MODEL, S), BF16), hshape, hshape, hshape, hshape],
        compiler_params=_cparams(("parallel",)), name="peerprep",
    )(h, lw["g2"], lw["wqt"], lw["keys"])


def _peer_kernel(h_ref, xt_ref, wu_ref, wvt_ref, th_ref, e1_ref, s2_ref, e2_ref, o_ref, acc_ref):
    e = pl.program_id(1)

    @pl.when(e == 0)
    def _():
        acc_ref[...] = jnp.zeros_like(acc_ref)

    a = jnp.dot(wu_ref[...], xt_ref[...], preferred_element_type=F32)
    act = 0.5 * a * (1.0 + lax.erf(a * (1.0 / math.sqrt(2.0))))
    gates = []
    for il in range(ET_PEER // PEER_N_KEYS):
        g = jnp.zeros((PEER_N_KEYS, a.shape[1]), F32)
        for h in range(PEER_HEADS):
            th = th_ref[h, il:il + 1, :]
            w1 = e1_ref[h, il:il + 1, :]
            g = g + jnp.where(s2_ref[h] >= th, e2_ref[h] * w1, 0.0)
        gates.append(g)
    ag = (act * jnp.concatenate(gates, axis=0)).astype(BF16)
    acc_ref[...] += jnp.dot(wvt_ref[...], ag, preferred_element_type=F32)

    @pl.when(e == pl.num_programs(1) - 1)
    def _():
        o_ref[...] = h_ref[...] + acc_ref[...].T


def _peer_call(h, xt, th, e1, s2, e2, lw):
    S = h.shape[0]
    tm = min(TM_PEER, S)
    n_exp = lw["wu"].shape[0]
    rows = ET_PEER // PEER_N_KEYS
    tile = pl.BlockSpec((PEER_HEADS, rows, tm), lambda j, e: (0, e, j))
    res = pl.BlockSpec((PEER_HEADS, PEER_N_KEYS, tm), lambda j, e: (0, 0, j))
    return pl.pallas_call(
        _peer_kernel, grid=(S // tm, n_exp // ET_PEER),
        in_specs=[pl.BlockSpec((tm, D_MODEL), lambda j, e: (j, 0)),
                  pl.BlockSpec((D_MODEL, tm), lambda j, e: (0, j)),
                  pl.BlockSpec((ET_PEER, D_MODEL), lambda j, e: (e, 0)),
                  pl.BlockSpec((D_MODEL, ET_PEER), lambda j, e: (0, e)),
                  tile, tile, res, res],
        out_specs=pl.BlockSpec((tm, D_MODEL), lambda j, e: (j, 0)),
        out_shape=jax.ShapeDtypeStruct((S, D_MODEL), F32),
        scratch_shapes=[pltpu.VMEM((D_MODEL, tm), F32)],
        compiler_params=_cparams(("parallel", "arbitrary")), name="peer",
    )(h, xt, lw["wu"], lw["wvt"], th, e1, s2, e2)


def _pad_heads(w, n_heads, width):
    lead = w.shape[:-1]
    w = w.reshape(*lead, n_heads, width)
    w = jnp.pad(w, [(0, 0)] * len(lead) + [(0, 0), (0, LANES - width)])
    return w.reshape(*lead, n_heads * LANES)


def _dup_kv(w):
    lead = w.shape[:-1]
    w = w.reshape(*lead, A_KV_HEADS, 1, HEAD_DIM)
    w = jnp.broadcast_to(w, (*lead, A_KV_HEADS, A_HEADS // A_KV_HEADS, HEAD_DIM))
    return w.reshape(*lead, A_HEADS * HEAD_DIM)


def _block_diag(width, segments):
    m = np.zeros((width, width), np.float32)
    for a, b, d in segments:
        m[a:b, a:b] = 1.0 / d
    return jnp.asarray(m, BF16)


def _consts():
    seg64 = [(i * 64, (i + 1) * 64, 64) for i in range(4)]
    seg128 = [(i * 128, (i + 1) * 128, 64) for i in range(2)]
    segc = []
    for i in range(2):
        segc += [(i * 128, i * 128 + 64, 64), (i * 128 + 64, (i + 1) * 128, 32)]
    return {
        "bd64": _block_diag(256, seg64), "bd128": _block_diag(256, seg128), "bdc": _block_diag(256, segc),
        "bd256": _block_diag(256, [(0, 256, 256)]), "bdkv": _block_diag(128, [(0, 128, 128)]),
    }


def _layer_weights(l, p):
    w = p["w_in"][l]
    cols = [w[:, IN_OFFS[i]:IN_OFFS[i + 1]] for i in range(len(IN_SPLITS))]
    aq, ak, av, bq, bk, bv, cq, ckv, ckr, dq, dk, dv, wgate = cols
    ckr_rep = jnp.zeros((D_MODEL, C_HEADS, LANES), F32).at[:, :, C_NOPE:C_QK_DIM].set(ckr[:, None, :])
    w1 = jnp.concatenate([
        aq, _dup_kv(ak), _dup_kv(av), bq, bk, bv, cq, ckv, ckr_rep.reshape(D_MODEL, C_HEADS * LANES),
        _pad_heads(dq, D_HEADS, HEAD_DIM), _pad_heads(dk, D_KV_HEADS, HEAD_DIM),
        _pad_heads(dv, D_KV_HEADS, HEAD_DIM)], axis=1).astype(BF16)

    scale = HEAD_DIM ** -0.5
    ones = lambda n: jnp.ones((n,), F32)
    ck_gain = p["c_k_gain"][l]
    ckr_gain = jnp.zeros((C_HEADS, LANES), F32).at[:, C_NOPE:C_QK_DIM].set(ck_gain[None, C_NOPE:])
    gains = jnp.concatenate([
        jnp.tile(p["a_q_gain"][l], 4) * scale, jnp.tile(p["a_k_gain"][l], 4), ones(256),
        jnp.tile(p["b_q_gain"][l], (1, 4)).reshape(-1) * scale, jnp.tile(p["b_k_gain"][l], (1, 4)).reshape(-1),
        ones(768),
        p["c_q_lat_gain"][l], p["c_kv_lat_gain"][l], ckr_gain.reshape(-1),
        jnp.tile(_pad_heads(p["d_q_gain"][l], 1, HEAD_DIM), 4) * scale,
        jnp.tile(_pad_heads(p["d_k_gain"][l], 1, HEAD_DIM), 2), ones(256)])[None, :]

    cq_gain = jnp.tile(_pad_heads(p["c_q_gain"][l], 1, C_QK_DIM), C_HEADS) * (C_QK_DIM ** -0.5)
    ckn_gain = jnp.tile(_pad_heads(ck_gain[:C_NOPE], 1, C_NOPE), C_HEADS)
    wukv = p["c_w_ukv"][l].reshape(C_KV_RANK, C_HEADS, C_NOPE + C_V)
    wukv = jnp.concatenate([_pad_heads(wukv[:, :, :C_NOPE].reshape(C_KV_RANK, -1), C_HEADS, C_NOPE),
                            _pad_heads(wukv[:, :, C_NOPE:].reshape(C_KV_RANK, -1), C_HEADS, C_V)], axis=1)

    wb = p["w_branch"][l]
    pad_rows = lambda m: _pad_heads(m.T, 4, HEAD_DIM).T
    return {
        "g1": p["norm1_gain"][l][None, :], "w1": w1, "gains": gains,
        "wuq": _pad_heads(p["c_w_uq"][l], C_HEADS, C_QK_DIM).astype(BF16), "wukv": wukv.astype(BF16),
        "gcq": cq_gain[None, :], "gckn": ckn_gain[None, :],
        "wg": wgate.astype(BF16), "gb": p["gate_bias"][l][None, :],
        "wba": wb[0].astype(BF16), "wbb": wb[1].astype(BF16),
        "wbc": pad_rows(wb[2]).astype(BF16), "wbd": pad_rows(wb[3]).astype(BF16),
        "wo": p["w_out"][l].astype(BF16),
        "g2": p["norm2_gain"][l][None, :], "wqt": p["peer_w_q"][l].T.astype(BF16),
        "keys": p["peer_sub_keys"][l].astype(BF16),
        "wu": p["peer_w_u"][l].astype(BF16), "wvt": p["peer_w_v"][l].T.astype(BF16),
    }


def _rope_tables(S):
    half = 16
    freqs = ROPE_THETA ** (-jnp.arange(half, dtype=F32) / half)

    def cs(pos):
        ang = pos.astype(F32)[:, None] * freqs[None, :]
        return jnp.cos(ang), jnp.sin(ang)

    def table(parts):
        cos = jnp.ones((S, LANES), F32)
        sa = jnp.zeros((S, LANES), F32)
        sb = jnp.zeros((S, LANES), F32)
        for start, (c, s) in parts:
            cos = cos.at[:, start:start + 2 * half].set(jnp.concatenate([c, c], axis=1))
            sa = sa.at[:, start:start + half].set(-s)
            sb = sb.at[:, start + half:start + 2 * half].set(s)
        return jnp.stack([cos, sa, sb])

    pos = jnp.arange(S, dtype=jnp.int32)
    tabc = table([(C_NOPE, cs(pos))])
    tabd = table([(0, cs(pos // GRID_W)), (2 * half, cs(pos % GRID_W))])
    return tabc, tabd


def _rel_bucket(rel):
    nb = REL_BUCKETS // 2
    max_exact = nb // 2
    ret = jnp.where(rel > 0, nb, 0)
    n = jnp.abs(rel)
    nf = jnp.maximum(n, 1).astype(F32)
    large = max_exact + (jnp.log(nf / max_exact) / math.log(REL_MAX_DIST / max_exact)
                         * (nb - max_exact)).astype(jnp.int32)
    large = jnp.minimum(large, nb - 1)
    return ret + jnp.where(n < max_exact, n, large)


def _band_bias(rel_table, half, dil):
    rel = jnp.arange(3 * QBLK)[None, :] - QBLK - jnp.arange(QBLK)[:, None]
    bias = rel_table.astype(F32)[_rel_bucket(rel * dil)].transpose(2, 0, 1)
    return jnp.where((jnp.abs(rel) <= half)[None], bias, NEG_INF)


def kernel(x, rel_bias, norm1_gain, w_in, gate_bias, a_q_gain, a_k_gain, a_sink, b_q_gain, b_k_gain, c_q_lat_gain, c_kv_lat_gain, c_w_uq, c_w_ukv, c_q_gain, c_k_gain, d_q_gain, d_k_gain, w_branch, w_out, norm2_gain, peer_w_q, peer_sub_keys, peer_w_u, peer_w_v):
    p = dict(rel_bias=rel_bias, norm1_gain=norm1_gain, w_in=w_in, gate_bias=gate_bias, a_q_gain=a_q_gain,
             a_k_gain=a_k_gain, a_sink=a_sink, b_q_gain=b_q_gain, b_k_gain=b_k_gain,
             c_q_lat_gain=c_q_lat_gain, c_kv_lat_gain=c_kv_lat_gain, c_w_uq=c_w_uq, c_w_ukv=c_w_ukv,
             c_q_gain=c_q_gain, c_k_gain=c_k_gain, d_q_gain=d_q_gain, d_k_gain=d_k_gain, w_branch=w_branch,
             w_out=w_out, norm2_gain=norm2_gain, peer_w_q=peer_w_q, peer_sub_keys=peer_sub_keys,
             peer_w_u=peer_w_u, peer_w_v=peer_w_v)
    Bn, S, D = x.shape
    assert Bn == 1 and D == D_MODEL and S % (16 * QBLK) == 0
    depth = w_in.shape[0]
    consts = _consts()
    tabc, tabd = _rope_tables(S)
    bias_a = _band_bias(rel_bias[:, :A_HEADS], A_HALF_WINDOW, 1)
    bias_b = [_band_bias(rel_bias[:, A_HEADS + g * B_HEADS:A_HEADS + (g + 1) * B_HEADS], win // (2 * dil), dil)
              for g, (win, dil) in enumerate(B_PATTERNS)]

    h = x.reshape(S, D)
    for l in range(depth):
        lw = _layer_weights(l, p)
        qa, ka, va, qb, kb, vb, qc, kct, vc, qd, kdt, vd = _proj_call(h, lw, consts, tabc, tabd)
        ya, _ = _banded_call(qa, ka, va, bias_a, a_sink[l], 1)
        obs, lses = [], []
        for g, (win, dil) in enumerate(B_PATTERNS):
            view = lambda t: t[:, g * 256:(g + 1) * 256].reshape(S // dil, dil * 256)
            o, lse = _banded_call(view(qb), view(kb), view(vb), bias_b[g], None, dil)
            obs.append(o.reshape(S, 256))
            lses.append(lse.reshape(S, 256))
        yc = _flash_call(qc, kct, vc, C_HEADS, 1)
        yd = _flash_call(qd, kdt, vd, D_HEADS, D_HEADS // D_KV_HEADS)
        h = _merge_call(h, lw, ya, obs, lses, yc, yd)
        xt, th, e1, s2, e2 = _peerprep_call(h, lw)
        h = _peer_call(h, xt, th, e1, s2, e2, lw)
    return h.reshape(Bn, S, D)
```

```python
import functools
import math

import jax
import jax.numpy as jnp
import numpy as np
from jax import lax
from jax.experimental import pallas as pl
from jax.experimental.pallas import tpu as pltpu

F32 = jnp.float32
BF16 = jnp.bfloat16

D_MODEL = 1024
HEAD_DIM = 64
GRID_W = 64
EPS = 1e-6
NEG_INF = -1e30
LOG2E = math.log2(math.e)
ROPE_THETA = 10000.0

A_HEADS, A_KV_HEADS, A_HALF_WINDOW = 4, 2, 128
B_PATTERNS = ((128, 1), (512, 4), (2048, 16))
B_GROUPS, B_HEADS = 3, 4
C_HEADS, C_Q_RANK, C_KV_RANK, C_NOPE, C_ROPE, C_V = 4, 256, 128, 64, 32, 64
C_QK_DIM = C_NOPE + C_ROPE
D_HEADS, D_KV_HEADS = 4, 2
N_BRANCHES = 4
REL_BUCKETS, REL_MAX_DIST = 32, 1024
PEER_HEADS, PEER_N_KEYS, PEER_KEY_DIM, PEER_TOPK = 8, 128, 128, 16
PEER_HALF = PEER_KEY_DIM // 2

IN_SPLITS = (256, 128, 128, 768, 768, 768, C_Q_RANK, C_KV_RANK, C_ROPE, 256, 128, 128, N_BRANCHES * D_MODEL)
IN_OFFS = tuple(int(o) for o in np.cumsum((0,) + IN_SPLITS))

LANES = 128
QBLK = 128
VMEM_LIMIT = 56 * 1024 * 1024

P_AQ, P_AK, P_AV = 0, 256, 512
P_BQ, P_BK, P_BV = 768, 1536, 2304
P_CQ, P_CKV, P_CKR = 3072, 3328, 3456
P_DQ, P_DK, P_DV = 3968, 4480, 4736
P_WIDTH = 4992

TM_PROJ = 256
TM_MERGE = 256
TQ_FLASH = 512
FLASH_CHUNKS = 2
TM_PREP = 256
TM_PEER = 256
ET_PEER = 1024


def _cparams(sem):
    return pltpu.CompilerParams(dimension_semantics=sem, vmem_limit_bytes=VMEM_LIMIT)


def _full(shape):
    n = len(shape)
    return pl.BlockSpec(shape, lambda *_: (0,) * n)


def _seg_norm(t, bd, gain):
    ms = jnp.dot((t * t).astype(BF16), bd, preferred_element_type=F32)
    return t * lax.rsqrt(ms + EPS) * gain


def _rope128(x, tab_ref):
    up = pltpu.roll(x, LANES - 16, axis=1)
    dn = pltpu.roll(x, 16, axis=1)
    return x * tab_ref[0] + up * tab_ref[1] + dn * tab_ref[2]


def _proj_kernel(h_ref, g1_ref, w_ref, gains_ref, bd64_ref, bd128_ref, bdc_ref, bd256_ref, bdkv_ref,
                 wuq_ref, wukv_ref, gcq_ref, gckn_ref, tabc_ref, tabd_ref,
                 qa_ref, ka_ref, va_ref, qb_ref, kb_ref, vb_ref,
                 qc_ref, kct_ref, vc_ref, qd_ref, kdt_ref, vd_ref):
    x = h_ref[...]
    ms = jnp.mean(x * x, axis=-1, keepdims=True)
    hn = (x * lax.rsqrt(ms + EPS) * g1_ref[...]).astype(BF16)

    def proj(off, width):
        return jnp.dot(hn, w_ref[:, off:off + width], preferred_element_type=F32)

    def gain(off, width):
        return gains_ref[:, off:off + width]

    bd64 = bd64_ref[...]
    bd128 = bd128_ref[...]
    bdc = bdc_ref[...]
    ones_col = gain(P_DV, 256)

    qa_ref[...] = _seg_norm(proj(P_AQ, 256), bd64, gain(P_AQ, 256)).astype(BF16)
    ka_ref[...] = _seg_norm(proj(P_AK, 256), bd64, gain(P_AK, 256)).astype(BF16)
    va_ref[...] = proj(P_AV, 256).astype(BF16)

    for g in range(B_GROUPS):
        o = g * 256
        qb_ref[:, o:o + 256] = _seg_norm(proj(P_BQ + o, 256), bd64, gain(P_BQ + o, 256)).astype(BF16)
        kb_ref[:, o:o + 256] = _seg_norm(proj(P_BK + o, 256), bd64, gain(P_BK + o, 256)).astype(BF16)
        vb_ref[:, o:o + 256] = proj(P_BV + o, 256).astype(BF16)

    cq = _seg_norm(proj(P_CQ, 256), bd256_ref[...], gain(P_CQ, 256)).astype(BF16)
    ckv = _seg_norm(proj(P_CKV, 128), bdkv_ref[...], gain(P_CKV, 128)).astype(BF16)
    for hb in range(2):
        o = hb * 256
        q = jnp.dot(cq, wuq_ref[:, o:o + 256], preferred_element_type=F32)
        q = _seg_norm(q, bdc, gcq_ref[:, o:o + 256])
        kn = jnp.dot(ckv, wukv_ref[:, o:o + 256], preferred_element_type=F32)
        kn = _seg_norm(kn, bd128, gckn_ref[:, o:o + 256])
        kp = _seg_norm(proj(P_CKR + o, 256), bdc, gain(P_CKR + o, 256))
        vc_ref[:, o:o + 256] = (jnp.dot(ckv, wukv_ref[:, 512 + o:512 + o + 256], preferred_element_type=F32)
                                + ones_col).astype(BF16)
        for j in range(2):
            c = j * LANES
            hd = hb * 2 + j
            qc_ref[:, o + c:o + c + LANES] = _rope128(q[:, c:c + LANES], tabc_ref).astype(BF16)
            k = kn[:, c:c + LANES] + _rope128(kp[:, c:c + LANES], tabc_ref)
            kct_ref[hd, 0] = k.T.astype(BF16)

    for hb in range(2):
        o = hb * 256
        q = _seg_norm(proj(P_DQ + o, 256), bd128, gain(P_DQ + o, 256))
        for j in range(2):
            c = j * LANES
            qd_ref[:, o + c:o + c + LANES] = _rope128(q[:, c:c + LANES], tabd_ref).astype(BF16)
    k = _seg_norm(proj(P_DK, 256), bd128, gain(P_DK, 256))
    for j in range(D_KV_HEADS):
        c = j * LANES
        kdt_ref[j, 0] = _rope128(k[:, c:c + LANES], tabd_ref).T.astype(BF16)
    vd_ref[...] = (proj(P_DV, 256) + ones_col).astype(BF16)


def _proj_call(h, lw, consts, tabc, tabd):
    S = h.shape[0]
    tm = TM_PROJ
    nt = S // tm
    row = lambda w: pl.BlockSpec((tm, w), lambda i: (i, 0))
    in_specs = [
        row(D_MODEL), _full((1, D_MODEL)), _full((D_MODEL, P_WIDTH)), _full((1, P_WIDTH)),
        _full((256, 256)), _full((256, 256)), _full((256, 256)), _full((256, 256)), _full((128, 128)),
        _full((C_Q_RANK, 512)), _full((C_KV_RANK, 1024)), _full((1, 512)), _full((1, 512)),
        pl.BlockSpec((3, tm, LANES), lambda i: (0, i, 0)),
        pl.BlockSpec((3, tm, LANES), lambda i: (0, i, 0)),
    ]
    out_shape = [
        jax.ShapeDtypeStruct((S, 256), BF16), jax.ShapeDtypeStruct((S, 256), BF16),
        jax.ShapeDtypeStruct((S, 256), BF16),
        jax.ShapeDtypeStruct((S, 768), BF16), jax.ShapeDtypeStruct((S, 768), BF16),
        jax.ShapeDtypeStruct((S, 768), BF16),
        jax.ShapeDtypeStruct((S, 512), BF16), jax.ShapeDtypeStruct((C_HEADS, nt, LANES, tm), BF16),
        jax.ShapeDtypeStruct((S, 512), BF16),
        jax.ShapeDtypeStruct((S, 512), BF16), jax.ShapeDtypeStruct((D_KV_HEADS, nt, LANES, tm), BF16),
        jax.ShapeDtypeStruct((S, 256), BF16),
    ]
    out_specs = [
        row(256), row(256), row(256), row(768), row(768), row(768),
        row(512), pl.BlockSpec((C_HEADS, 1, LANES, tm), lambda i: (0, i, 0, 0)), row(512),
        row(512), pl.BlockSpec((D_KV_HEADS, 1, LANES, tm), lambda i: (0, i, 0, 0)), row(256),
    ]
    return pl.pallas_call(
        _proj_kernel, grid=(nt,), in_specs=in_specs, out_specs=out_specs, out_shape=out_shape,
        compiler_params=_cparams(("parallel",)), name="proj",
    )(h, lw["g1"], lw["w1"], lw["gains"], consts["bd64"], consts["bd128"], consts["bdc"],
      consts["bd256"], consts["bdkv"], lw["wuq"], lw["wukv"], lw["gcq"], lw["gckn"], tabc, tabd)


def _banded_kernel(*refs, seq_len, has_sink):
    if has_sink:
        sink_ref, refs = refs[0], refs[1:]
    q_ref, kp_ref, kc_ref, kn_ref, vp_ref, vc_ref, vn_ref, bias_ref, o_ref, lse_ref = refs
    i = pl.program_id(1)
    q = q_ref[...]
    k = jnp.concatenate([kp_ref[...], kc_ref[...], kn_ref[...]], axis=0)
    v = jnp.concatenate([vp_ref[...], vc_ref[...], vn_ref[...]], axis=0)
    kpos = (i - 1) * QBLK + lax.broadcasted_iota(jnp.int32, (1, 3 * QBLK), 1)
    valid = (kpos >= 0) & (kpos < seq_len)
    lane = lax.broadcasted_iota(jnp.int32, (1, 4 * HEAD_DIM), 1)
    out = jnp.zeros((QBLK, 4 * HEAD_DIM), F32)
    lse = jnp.zeros((QBLK, 4 * HEAD_DIM), F32)
    for h in range(4):
        hmask = (lane >= h * HEAD_DIM) & (lane < (h + 1) * HEAD_DIM)
        qh = jnp.where(hmask, q, jnp.zeros_like(q))
        s = lax.dot_general(qh, k, (((1,), (1,)), ((), ())), preferred_element_type=F32)
        s = jnp.where(valid, s + bias_ref[h], NEG_INF)
        m = jnp.max(s, axis=-1, keepdims=True)
        if has_sink:
            m = jnp.maximum(m, sink_ref[h])
        p = jnp.exp(s - m)
        denom = jnp.sum(p, axis=-1, keepdims=True)
        if has_sink:
            denom = denom + jnp.exp(sink_ref[h] - m)
        pv = jnp.dot(p.astype(BF16), v, preferred_element_type=F32)
        out = out + jnp.where(hmask, pv / denom, 0.0)
        lse = lse + jnp.where(hmask, m + jnp.log(denom), 0.0)
    o_ref[...] = out.astype(o_ref.dtype)
    lse_ref[...] = lse


def _banded_call(q, k, v, bias, sink, dil):
    L = q.shape[0]
    nb = L // QBLK
    blk = lambda f: pl.BlockSpec((QBLK, 256), f)
    cur = lambda r, i: (i, r)
    prev = lambda r, i: (jnp.maximum(i - 1, 0), r)
    nxt = lambda r, i: (jnp.minimum(i + 1, nb - 1), r)
    in_specs = [blk(cur), blk(prev), blk(cur), blk(nxt), blk(prev), blk(cur), blk(nxt),
                _full((4, QBLK, 3 * QBLK))]
    args = [q, k, k, k, v, v, v, bias]
    if sink is not None:
        in_specs = [pl.BlockSpec(memory_space=pltpu.SMEM)] + in_specs
        args = [sink] + args
    return pl.pallas_call(
        functools.partial(_banded_kernel, seq_len=L, has_sink=sink is not None),
        grid=(dil, nb), in_specs=in_specs,
        out_specs=[blk(cur), blk(cur)],
        out_shape=[jax.ShapeDtypeStruct(q.shape, BF16), jax.ShapeDtypeStruct(q.shape, F32)],
        compiler_params=_cparams(("parallel", "parallel")), name="banded",
    )(*args)


def _flash_kernel(q_ref, kt_ref, v_ref, o_ref, s_buf, p_buf, a_buf, m_ref, acc_ref, *, n_chunks, chunk):
    per = FLASH_CHUNKS
    n_steps = n_chunks // per
    width = per * chunk

    def scores(t, slot):
        kt = jnp.concatenate([kt_ref[t * per + j] for j in range(per)], axis=1)
        s_buf[slot] = jnp.dot(q_ref[...], kt, preferred_element_type=F32)

    def softmax(slot):
        s = s_buf[slot]
        m = m_ref[...]
        m_new = jnp.maximum(m, jnp.max(s, axis=-1, keepdims=True))
        m_ref[...] = m_new
        a_buf[slot] = jnp.exp2(m - m_new)
        p_buf[slot] = jnp.exp2(s - m_new).astype(BF16)

    def accumulate(t, slot):
        vv = v_ref[pl.ds(pl.multiple_of(t * width, width), width), :]
        acc_ref[...] = a_buf[slot] * acc_ref[...] + jnp.dot(p_buf[slot], vv, preferred_element_type=F32)

    m_ref[...] = jnp.full(m_ref.shape, -jnp.inf, F32)
    acc_ref[...] = jnp.zeros(acc_ref.shape, F32)
    scores(0, 0)
    scores(1, 1)
    softmax(0)

    def body(i, carry):
        t = 2 * i
        scores(t, 0)
        softmax(1)
        accumulate(t - 2, 0)
        scores(t + 1, 1)
        softmax(0)
        accumulate(t - 1, 1)
        return carry

    lax.fori_loop(1, n_steps // 2, body, 0)
    softmax(1)
    accumulate(n_steps - 2, 0)
    accumulate(n_steps - 1, 1)
    acc = acc_ref[...]
    lane = lax.broadcasted_iota(jnp.int32, (1, LANES), 1)
    o_ref[...] = jnp.where(lane < HEAD_DIM, acc / acc[:, HEAD_DIM:HEAD_DIM + 1], 0.0).astype(o_ref.dtype)


def _flash_call(q, kt, v, n_heads, group):
    S = q.shape[0]
    n_chunks, chunk = kt.shape[1], kt.shape[3]
    tq = min(TQ_FLASH, S)
    width = FLASH_CHUNKS * chunk
    assert n_chunks % (2 * FLASH_CHUNKS) == 0 and n_chunks >= 4 * FLASH_CHUNKS
    return pl.pallas_call(
        functools.partial(_flash_kernel, n_chunks=n_chunks, chunk=chunk),
        grid=(n_heads, S // tq),
        in_specs=[pl.BlockSpec((tq, LANES), lambda h, i: (i, h)),
                  pl.BlockSpec((None, n_chunks, LANES, chunk), lambda h, i: (h // group, 0, 0, 0)),
                  pl.BlockSpec((S, LANES), lambda h, i: (0, h // group))],
        out_specs=pl.BlockSpec((tq, LANES), lambda h, i: (i, h)),
        out_shape=jax.ShapeDtypeStruct((S, n_heads * LANES), BF16),
        scratch_shapes=[pltpu.VMEM((2, tq, width), F32), pltpu.VMEM((2, tq, width), BF16),
                        pltpu.VMEM((2, tq, 1), F32), pltpu.VMEM((tq, 1), F32), pltpu.VMEM((tq, LANES), F32)],
        compiler_params=_cparams(("parallel", "parallel")), name="flash",
    )(q, kt, v)


def _merge_kernel(h_ref, g1_ref, wg_ref, gb_ref, ya_ref, ob1_ref, ob2_ref, ob3_ref,
                  ls1_ref, ls2_ref, ls3_ref, yc_ref, yd_ref,
                  wba_ref, wbb_ref, wbc_ref, wbd_ref, wo_ref, o_ref):
    x = h_ref[...]
    ms = jnp.mean(x * x, axis=-1, keepdims=True)
    hn = (x * lax.rsqrt(ms + EPS) * g1_ref[...]).astype(BF16)

    l1, l2, l3 = ls1_ref[...], ls2_ref[...], ls3_ref[...]
    mx = jnp.maximum(jnp.maximum(l1, l2), l3)
    e1, e2, e3 = jnp.exp(l1 - mx), jnp.exp(l2 - mx), jnp.exp(l3 - mx)
    yb = (e1 * ob1_ref[...].astype(F32) + e2 * ob2_ref[...].astype(F32)
          + e3 * ob3_ref[...].astype(F32)) / (e1 + e2 + e3)

    ys = (ya_ref[...], yb.astype(BF16), yc_ref[...], yd_ref[...])
    wbs = (wba_ref, wbb_ref, wbc_ref, wbd_ref)
    merged = jnp.zeros((x.shape[0], D_MODEL), F32)
    for n in range(N_BRANCHES):
        o = n * D_MODEL
        gpre = jnp.dot(hn, wg_ref[:, o:o + D_MODEL], preferred_element_type=F32) + gb_ref[:, o:o + D_MODEL]
        br = jnp.dot(ys[n], wbs[n][...], preferred_element_type=F32)
        merged = merged + jax.nn.sigmoid(gpre) * br
    o_ref[...] = x + jnp.dot(merged.astype(BF16), wo_ref[...], preferred_element_type=F32)


def _merge_call(h, lw, ya, obs, lses, yc, yd):
    S = h.shape[0]
    tm = TM_MERGE
    row = lambda w: pl.BlockSpec((tm, w), lambda i: (i, 0))
    in_specs = [row(D_MODEL), _full((1, D_MODEL)), _full((D_MODEL, 4 * D_MODEL)), _full((1, 4 * D_MODEL)),
                row(256), row(256), row(256), row(256), row(256), row(256), row(256), row(512), row(512),
                _full((256, D_MODEL)), _full((256, D_MODEL)), _full((512, D_MODEL)), _full((512, D_MODEL)),
                _full((D_MODEL, D_MODEL))]
    return pl.pallas_call(
        _merge_kernel, grid=(S // tm,), in_specs=in_specs, out_specs=row(D_MODEL),
        out_shape=jax.ShapeDtypeStruct((S, D_MODEL), F32),
        compiler_params=_cparams(("parallel",)), name="merge",
    )(h, lw["g1"], lw["wg"], lw["gb"], ya, obs[0], obs[1], obs[2], lses[0], lses[1], lses[2], yc, yd,
      lw["wba"], lw["wbb"], lw["wbc"], lw["wbd"], lw["wo"])


def _top16_desc(s):
    vals = []
    cur = s
    for _ in range(PEER_TOPK):
        m = jnp.max(cur, axis=0, keepdims=True)
        vals.append(m)
        cur = jnp.where(cur == m, -jnp.inf, cur)
    return vals


def _stack_rows(rows):
    n = len(rows)
    ridx = lax.broadcasted_iota(jnp.int32, (n, rows[0].shape[1]), 0)
    out = jnp.zeros((n, rows[0].shape[1]), F32)
    for r, v in enumerate(rows):
        out = jnp.where(ridx == r, v, out)
    return out


def _peerprep_kernel(h_ref, g2_ref, wqt_ref, keys_ref, xt_ref, th_ref, e1_ref, s2_ref, e2_ref):
    x = h_ref[...]
    ms = jnp.mean(x * x, axis=-1, keepdims=True)
    xn = x * lax.rsqrt(ms + EPS) * g2_ref[...]
    xt = xn.T.astype(BF16)
    xt_ref[...] = xt
    qt = jnp.dot(wqt_ref[...], xt, preferred_element_type=F32).astype(BF16)
    for h in range(PEER_HEADS):
        o = h * PEER_KEY_DIM
        s1 = jnp.dot(keys_ref[h, 0], qt[o:o + PEER_HALF], preferred_element_type=F32)
        s2 = jnp.dot(keys_ref[h, 1], qt[o + PEER_HALF:o + PEER_KEY_DIM], preferred_element_type=F32)
        v1 = _top16_desc(s1)
        v2 = _stack_rows(_top16_desc(s2))
        cand = jnp.concatenate([v1[a] + v2 for a in range(PEER_TOPK)], axis=0)
        top = _top16_desc(cand)
        t0, tau = top[0], top[PEER_TOPK - 1]
        z = top[0] - t0
        z = jnp.exp(z)
        for t in top[1:]:
            z = z + jnp.exp(t - t0)
        v20 = v2[0:1]
        th_ref[h] = tau - s1
        e1_ref[h] = jnp.exp(s1 - v1[0]) / z
        s2_ref[h] = s2
        e2_ref[h] = jnp.exp(s2 - v20)


def _peerprep_call(h, lw):
    S = h.shape[0]
    tm = min(TM_PREP, S)
    hs = pl.BlockSpec((PEER_HEADS, PEER_N_KEYS, tm), lambda i: (0, 0, i))
    hshape = jax.ShapeDtypeStruct((PEER_HEADS, PEER_N_KEYS, S), F32)
    return pl.pallas_call(
        _peerprep_kernel, grid=(S // tm,),
        in_specs=[pl.BlockSpec((tm, D_MODEL), lambda i: (i, 0)), _full((1, D_MODEL)),
                  _full((D_MODEL, D_MODEL)), _full((PEER_HEADS, 2, PEER_N_KEYS, PEER_HALF))],
        out_specs=[pl.BlockSpec((D_MODEL, tm), lambda i: (0, i)), hs, hs, hs, hs],
        out_shape=[jax.ShapeDtypeStruct((D_MODEL, S), BF16), hshape, hshape, hshape, hshape],
        compiler_params=_cparams(("parallel",)), name="peerprep",
    )(h, lw["g2"], lw["wqt"], lw["keys"])


def _peer_kernel(h_ref, xt_ref, wu_ref, wvt_ref, th_ref, e1_ref, s2_ref, e2_ref, o_ref, acc_ref):
    e = pl.program_id(1)

    @pl.when(e == 0)
    def _():
        acc_ref[...] = jnp.zeros_like(acc_ref)

    a = jnp.dot(wu_ref[...], xt_ref[...], preferred_element_type=F32)
    act = 0.5 * a * (1.0 + lax.erf(a * (1.0 / math.sqrt(2.0))))
    gates = []
    for il in range(ET_PEER // PEER_N_KEYS):
        g = jnp.zeros((PEER_N_KEYS, a.shape[1]), F32)
        for h in range(PEER_HEADS):
            th = th_ref[h, il:il + 1, :]
            w1 = e1_ref[h, il:il + 1, :]
            g = g + jnp.where(s2_ref[h] >= th, e2_ref[h] * w1, 0.0)
        gates.append(g)
    ag = (act * jnp.concatenate(gates, axis=0)).astype(BF16)
    acc_ref[...] += jnp.dot(wvt_ref[...], ag, preferred_element_type=F32)

    @pl.when(e == pl.num_programs(1) - 1)
    def _():
        o_ref[...] = h_ref[...] + acc_ref[...].T


def _peer_call(h, xt, th, e1, s2, e2, lw):
    S = h.shape[0]
    tm = min(TM_PEER, S)
    n_exp = lw["wu"].shape[0]
    rows = ET_PEER // PEER_N_KEYS
    tile = pl.BlockSpec((PEER_HEADS, rows, tm), lambda j, e: (0, e, j))
    res = pl.BlockSpec((PEER_HEADS, PEER_N_KEYS, tm), lambda j, e: (0, 0, j))
    return pl.pallas_call(
        _peer_kernel, grid=(S // tm, n_exp // ET_PEER),
        in_specs=[pl.BlockSpec((tm, D_MODEL), lambda j, e: (j, 0)),
                  pl.BlockSpec((D_MODEL, tm), lambda j, e: (0, j)),
                  pl.BlockSpec((ET_PEER, D_MODEL), lambda j, e: (e, 0)),
                  pl.BlockSpec((D_MODEL, ET_PEER), lambda j, e: (0, e)),
                  tile, tile, res, res],
        out_specs=pl.BlockSpec((tm, D_MODEL), lambda j, e: (j, 0)),
        out_shape=jax.ShapeDtypeStruct((S, D_MODEL), F32),
        scratch_shapes=[pltpu.VMEM((D_MODEL, tm), F32)],
        compiler_params=_cparams(("parallel", "arbitrary")), name="peer",
    )(h, xt, lw["wu"], lw["wvt"], th, e1, s2, e2)


def _pad_heads(w, n_heads, width):
    lead = w.shape[:-1]
    w = w.reshape(*lead, n_heads, width)
    w = jnp.pad(w, [(0, 0)] * len(lead) + [(0, 0), (0, LANES - width)])
    return w.reshape(*lead, n_heads * LANES)


def _dup_kv(w):
    lead = w.shape[:-1]
    w = w.reshape(*lead, A_KV_HEADS, 1, HEAD_DIM)
    w = jnp.broadcast_to(w, (*lead, A_KV_HEADS, A_HEADS // A_KV_HEADS, HEAD_DIM))
    return w.reshape(*lead, A_HEADS * HEAD_DIM)


def _block_diag(width, segments):
    m = np.zeros((width, width), np.float32)
    for a, b, d in segments:
        m[a:b, a:b] = 1.0 / d
    return jnp.asarray(m, BF16)


def _consts():
    seg64 = [(i * 64, (i + 1) * 64, 64) for i in range(4)]
    seg128 = [(i * 128, (i + 1) * 128, 64) for i in range(2)]
    segc = []
    for i in range(2):
        segc += [(i * 128, i * 128 + 64, 64), (i * 128 + 64, (i + 1) * 128, 32)]
    return {
        "bd64": _block_diag(256, seg64), "bd128": _block_diag(256, seg128), "bdc": _block_diag(256, segc),
        "bd256": _block_diag(256, [(0, 256, 256)]), "bdkv": _block_diag(128, [(0, 128, 128)]),
    }


def _layer_weights(l, p):
    w = p["w_in"][l]
    cols = [w[:, IN_OFFS[i]:IN_OFFS[i + 1]] for i in range(len(IN_SPLITS))]
    aq, ak, av, bq, bk, bv, cq, ckv, ckr, dq, dk, dv, wgate = cols
    ckr_rep = jnp.zeros((D_MODEL, C_HEADS, LANES), F32).at[:, :, C_NOPE:C_QK_DIM].set(ckr[:, None, :])
    w1 = jnp.concatenate([
        aq, _dup_kv(ak), _dup_kv(av), bq, bk, bv, cq, ckv, ckr_rep.reshape(D_MODEL, C_HEADS * LANES),
        _pad_heads(dq, D_HEADS, HEAD_DIM), _pad_heads(dk, D_KV_HEADS, HEAD_DIM),
        _pad_heads(dv, D_KV_HEADS, HEAD_DIM)], axis=1).astype(BF16)

    scale = HEAD_DIM ** -0.5
    ones = lambda n: jnp.ones((n,), F32)
    ones_col = jnp.zeros((2, LANES), F32).at[:, HEAD_DIM].set(1.0).reshape(-1)
    ck_gain = p["c_k_gain"][l]
    ckr_gain = jnp.zeros((C_HEADS, LANES), F32).at[:, C_NOPE:C_QK_DIM].set(ck_gain[None, C_NOPE:])
    gains = jnp.concatenate([
        jnp.tile(p["a_q_gain"][l], 4) * scale, jnp.tile(p["a_k_gain"][l], 4), ones(256),
        jnp.tile(p["b_q_gain"][l], (1, 4)).reshape(-1) * scale, jnp.tile(p["b_k_gain"][l], (1, 4)).reshape(-1),
        ones(768),
        p["c_q_lat_gain"][l], p["c_kv_lat_gain"][l], ckr_gain.reshape(-1),
        jnp.tile(_pad_heads(p["d_q_gain"][l], 1, HEAD_DIM), 4) * (scale * LOG2E),
        jnp.tile(_pad_heads(p["d_k_gain"][l], 1, HEAD_DIM), 2), ones_col])[None, :]

    cq_gain = jnp.tile(_pad_heads(p["c_q_gain"][l], 1, C_QK_DIM), C_HEADS) * (C_QK_DIM ** -0.5 * LOG2E)
    ckn_gain = jnp.tile(_pad_heads(ck_gain[:C_NOPE], 1, C_NOPE), C_HEADS)
    wukv = p["c_w_ukv"][l].reshape(C_KV_RANK, C_HEADS, C_NOPE + C_V)
    wukv = jnp.concatenate([_pad_heads(wukv[:, :, :C_NOPE].reshape(C_KV_RANK, -1), C_HEADS, C_NOPE),
                            _pad_heads(wukv[:, :, C_NOPE:].reshape(C_KV_RANK, -1), C_HEADS, C_V)], axis=1)

    wb = p["w_branch"][l]
    pad_rows = lambda m: _pad_heads(m.T, 4, HEAD_DIM).T
    return {
        "g1": p["norm1_gain"][l][None, :], "w1": w1, "gains": gains,
        "wuq": _pad_heads(p["c_w_uq"][l], C_HEADS, C_QK_DIM).astype(BF16), "wukv": wukv.astype(BF16),
        "gcq": cq_gain[None, :], "gckn": ckn_gain[None, :],
        "wg": wgate.astype(BF16), "gb": p["gate_bias"][l][None, :],
        "wba": wb[0].astype(BF16), "wbb": wb[1].astype(BF16),
        "wbc": pad_rows(wb[2]).astype(BF16), "wbd": pad_rows(wb[3]).astype(BF16),
        "wo": p["w_out"][l].astype(BF16),
        "g2": p["norm2_gain"][l][None, :], "wqt": p["peer_w_q"][l].T.astype(BF16),
        "keys": p["peer_sub_keys"][l].astype(BF16),
        "wu": p["peer_w_u"][l].astype(BF16), "wvt": p["peer_w_v"][l].T.astype(BF16),
    }


def _rope_tables(S):
    half = 16
    freqs = ROPE_THETA ** (-jnp.arange(half, dtype=F32) / half)

    def cs(pos):
        ang = pos.astype(F32)[:, None] * freqs[None, :]
        return jnp.cos(ang), jnp.sin(ang)

    def table(parts):
        cos = jnp.ones((S, LANES), F32)
        sa = jnp.zeros((S, LANES), F32)
        sb = jnp.zeros((S, LANES), F32)
        for start, (c, s) in parts:
            cos = cos.at[:, start:start + 2 * half].set(jnp.concatenate([c, c], axis=1))
            sa = sa.at[:, start:start + half].set(-s)
            sb = sb.at[:, start + half:start + 2 * half].set(s)
        return jnp.stack([cos, sa, sb])

    pos = jnp.arange(S, dtype=jnp.int32)
    tabc = table([(C_NOPE, cs(pos))])
    tabd = table([(0, cs(pos // GRID_W)), (2 * half, cs(pos % GRID_W))])
    return tabc, tabd


def _rel_bucket(rel):
    nb = REL_BUCKETS // 2
    max_exact = nb // 2
    ret = jnp.where(rel > 0, nb, 0)
    n = jnp.abs(rel)
    nf = jnp.maximum(n, 1).astype(F32)
    large = max_exact + (jnp.log(nf / max_exact) / math.log(REL_MAX_DIST / max_exact)
                         * (nb - max_exact)).astype(jnp.int32)
    large = jnp.minimum(large, nb - 1)
    return ret + jnp.where(n < max_exact, n, large)


def _band_bias(rel_table, half, dil):
    rel = jnp.arange(3 * QBLK)[None, :] - QBLK - jnp.arange(QBLK)[:, None]
    bias = rel_table.astype(F32)[_rel_bucket(rel * dil)].transpose(2, 0, 1)
    return jnp.where((jnp.abs(rel) <= half)[None], bias, NEG_INF)


def kernel(x, rel_bias, norm1_gain, w_in, gate_bias, a_q_gain, a_k_gain, a_sink, b_q_gain, b_k_gain, c_q_lat_gain, c_kv_lat_gain, c_w_uq, c_w_ukv, c_q_gain, c_k_gain, d_q_gain, d_k_gain, w_branch, w_out, norm2_gain, peer_w_q, peer_sub_keys, peer_w_u, peer_w_v):
    p = dict(rel_bias=rel_bias, norm1_gain=norm1_gain, w_in=w_in, gate_bias=gate_bias, a_q_gain=a_q_gain,
             a_k_gain=a_k_gain, a_sink=a_sink, b_q_gain=b_q_gain, b_k_gain=b_k_gain,
             c_q_lat_gain=c_q_lat_gain, c_kv_lat_gain=c_kv_lat_gain, c_w_uq=c_w_uq, c_w_ukv=c_w_ukv,
             c_q_gain=c_q_gain, c_k_gain=c_k_gain, d_q_gain=d_q_gain, d_k_gain=d_k_gain, w_branch=w_branch,
             w_out=w_out, norm2_gain=norm2_gain, peer_w_q=peer_w_q, peer_sub_keys=peer_sub_keys,
             peer_w_u=peer_w_u, peer_w_v=peer_w_v)
    Bn, S, D = x.shape
    assert Bn == 1 and D == D_MODEL and S % (16 * QBLK) == 0
    depth = w_in.shape[0]
    consts = _consts()
    tabc, tabd = _rope_tables(S)
    bias_a = _band_bias(rel_bias[:, :A_HEADS], A_HALF_WINDOW, 1)
    bias_b = [_band_bias(rel_bias[:, A_HEADS + g * B_HEADS:A_HEADS + (g + 1) * B_HEADS], win // (2 * dil), dil)
              for g, (win, dil) in enumerate(B_PATTERNS)]

    h = x.reshape(S, D)
    for l in range(depth):
        lw = _layer_weights(l, p)
        qa, ka, va, qb, kb, vb, qc, kct, vc, qd, kdt, vd = _proj_call(h, lw, consts, tabc, tabd)
        ya, _ = _banded_call(qa, ka, va, bias_a, a_sink[l], 1)
        obs, lses = [], []
        for g, (win, dil) in enumerate(B_PATTERNS):
            view = lambda t: t[:, g * 256:(g + 1) * 256].reshape(S // dil, dil * 256)
            o, lse = _banded_call(view(qb), view(kb), view(vb), bias_b[g], None, dil)
            obs.append(o.reshape(S, 256))
            lses.append(lse.reshape(S, 256))
        yc = _flash_call(qc, kct, vc, C_HEADS, 1)
        yd = _flash_call(qd, kdt, vd, D_HEADS, D_HEADS // D_KV_HEADS)
        h = _merge_call(h, lw, ya, obs, lses, yc, yd)
        xt, th, e1, s2, e2 = _peerprep_call(h, lw)
        h = _peer_call(h, xt, th, e1, s2, e2, lw)
    return h.reshape(Bn, S, D)
```

```python
import functools
import math

import jax
import jax.numpy as jnp
import numpy as np
from jax import lax
from jax.experimental import pallas as pl
from jax.experimental.pallas import tpu as pltpu

F32 = jnp.float32
BF16 = jnp.bfloat16

D_MODEL = 1024
HEAD_DIM = 64
GRID_W = 64
EPS = 1e-6
NEG_INF = -1e30
LOG2E = math.log2(math.e)
ROPE_THETA = 10000.0

A_HEADS, A_KV_HEADS, A_HALF_WINDOW = 4, 2, 128
B_PATTERNS = ((128, 1), (512, 4), (2048, 16))
B_GROUPS, B_HEADS = 3, 4
C_HEADS, C_Q_RANK, C_KV_RANK, C_NOPE, C_ROPE, C_V = 4, 256, 128, 64, 32, 64
C_QK_DIM = C_NOPE + C_ROPE
D_HEADS, D_KV_HEADS = 4, 2
N_BRANCHES = 4
REL_BUCKETS, REL_MAX_DIST = 32, 1024
PEER_HEADS, PEER_N_KEYS, PEER_KEY_DIM, PEER_TOPK = 8, 128, 128, 16
PEER_HALF = PEER_KEY_DIM // 2

IN_SPLITS = (256, 128, 128, 768, 768, 768, C_Q_RANK, C_KV_RANK, C_ROPE, 256, 128, 128, N_BRANCHES * D_MODEL)
IN_OFFS = tuple(int(o) for o in np.cumsum((0,) + IN_SPLITS))

LANES = 128
QBLK = 128
VMEM_LIMIT = 56 * 1024 * 1024

P_AQ, P_AK, P_AV = 0, 256, 512
P_BQ, P_BK, P_BV = 768, 1536, 2304
P_CQ, P_CKV, P_CKR = 3072, 3328, 3456
P_DQ, P_DK, P_DV = 3968, 4480, 4736
P_WIDTH = 4992

TM_PROJ = 256
TM_MERGE = 256
TQ_FLASH = 512
FLASH_CHUNKS = 4
TM_PREP = 256
TM_PEER = 512
PEER_ROW_TILE = 16
ET_PEER = 1024


def _cparams(sem):
    return pltpu.CompilerParams(dimension_semantics=sem, vmem_limit_bytes=VMEM_LIMIT)


def _full(shape):
    n = len(shape)
    return pl.BlockSpec(shape, lambda *_: (0,) * n)


def _seg_norm(t, bd, gain):
    ms = jnp.dot((t * t).astype(BF16), bd, preferred_element_type=F32)
    return t * lax.rsqrt(ms + EPS) * gain


def _rope128(x, tab_ref):
    up = pltpu.roll(x, LANES - 16, axis=1)
    dn = pltpu.roll(x, 16, axis=1)
    return x * tab_ref[0] + up * tab_ref[1] + dn * tab_ref[2]


def _proj_kernel(h_ref, g1_ref, w_ref, gains_ref, bd64_ref, bd128_ref, bdc_ref, bd256_ref, bdkv_ref,
                 wuq_ref, wukv_ref, gcq_ref, gckn_ref, tabc_ref, tabd_ref,
                 qa_ref, ka_ref, va_ref, qb_ref, kb_ref, vb_ref,
                 qc_ref, kct_ref, vc_ref, qd_ref, kdt_ref, vd_ref):
    x = h_ref[...]
    ms = jnp.mean(x * x, axis=-1, keepdims=True)
    hn = (x * lax.rsqrt(ms + EPS) * g1_ref[...]).astype(BF16)

    def proj(off, width):
        return jnp.dot(hn, w_ref[:, off:off + width], preferred_element_type=F32)

    def gain(off, width):
        return gains_ref[:, off:off + width]

    bd64 = bd64_ref[...]
    bd128 = bd128_ref[...]
    bdc = bdc_ref[...]
    ones_col = gain(P_DV, 256)

    qa_ref[...] = _seg_norm(proj(P_AQ, 256), bd64, gain(P_AQ, 256)).astype(BF16)
    ka_ref[...] = _seg_norm(proj(P_AK, 256), bd64, gain(P_AK, 256)).astype(BF16)
    va_ref[...] = proj(P_AV, 256).astype(BF16)

    for g in range(B_GROUPS):
        o = g * 256
        qb_ref[:, o:o + 256] = _seg_norm(proj(P_BQ + o, 256), bd64, gain(P_BQ + o, 256)).astype(BF16)
        kb_ref[:, o:o + 256] = _seg_norm(proj(P_BK + o, 256), bd64, gain(P_BK + o, 256)).astype(BF16)
        vb_ref[:, o:o + 256] = proj(P_BV + o, 256).astype(BF16)

    cq = _seg_norm(proj(P_CQ, 256), bd256_ref[...], gain(P_CQ, 256)).astype(BF16)
    ckv = _seg_norm(proj(P_CKV, 128), bdkv_ref[...], gain(P_CKV, 128)).astype(BF16)
    for hb in range(2):
        o = hb * 256
        q = jnp.dot(cq, wuq_ref[:, o:o + 256], preferred_element_type=F32)
        q = _seg_norm(q, bdc, gcq_ref[:, o:o + 256])
        kn = jnp.dot(ckv, wukv_ref[:, o:o + 256], preferred_element_type=F32)
        kn = _seg_norm(kn, bd128, gckn_ref[:, o:o + 256])
        kp = _seg_norm(proj(P_CKR + o, 256), bdc, gain(P_CKR + o, 256))
        vc_ref[:, o:o + 256] = (jnp.dot(ckv, wukv_ref[:, 512 + o:512 + o + 256], preferred_element_type=F32)
                                + ones_col).astype(BF16)
        for j in range(2):
            c = j * LANES
            hd = hb * 2 + j
            qc_ref[:, o + c:o + c + LANES] = _rope128(q[:, c:c + LANES], tabc_ref).astype(BF16)
            k = kn[:, c:c + LANES] + _rope128(kp[:, c:c + LANES], tabc_ref)
            kct_ref[hd, 0] = k.T.astype(BF16)

    for hb in range(2):
        o = hb * 256
        q = _seg_norm(proj(P_DQ + o, 256), bd128, gain(P_DQ + o, 256))
        for j in range(2):
            c = j * LANES
            qd_ref[:, o + c:o + c + LANES] = _rope128(q[:, c:c + LANES], tabd_ref).astype(BF16)
    k = _seg_norm(proj(P_DK, 256), bd128, gain(P_DK, 256))
    for j in range(D_KV_HEADS):
        c = j * LANES
        kdt_ref[j, 0] = _rope128(k[:, c:c + LANES], tabd_ref).T.astype(BF16)
    vd_ref[...] = (proj(P_DV, 256) + ones_col).astype(BF16)


def _proj_call(h, lw, consts, tabc, tabd):
    S = h.shape[0]
    tm = TM_PROJ
    nt = S // tm
    row = lambda w: pl.BlockSpec((tm, w), lambda i: (i, 0))
    in_specs = [
        row(D_MODEL), _full((1, D_MODEL)), _full((D_MODEL, P_WIDTH)), _full((1, P_WIDTH)),
        _full((256, 256)), _full((256, 256)), _full((256, 256)), _full((256, 256)), _full((128, 128)),
        _full((C_Q_RANK, 512)), _full((C_KV_RANK, 1024)), _full((1, 512)), _full((1, 512)),
        pl.BlockSpec((3, tm, LANES), lambda i: (0, i, 0)),
        pl.BlockSpec((3, tm, LANES), lambda i: (0, i, 0)),
    ]
    out_shape = [
        jax.ShapeDtypeStruct((S, 256), BF16), jax.ShapeDtypeStruct((S, 256), BF16),
        jax.ShapeDtypeStruct((S, 256), BF16),
        jax.ShapeDtypeStruct((S, 768), BF16), jax.ShapeDtypeStruct((S, 768), BF16),
        jax.ShapeDtypeStruct((S, 768), BF16),
        jax.ShapeDtypeStruct((S, 512), BF16), jax.ShapeDtypeStruct((C_HEADS, nt, LANES, tm), BF16),
        jax.ShapeDtypeStruct((S, 512), BF16),
        jax.ShapeDtypeStruct((S, 512), BF16), jax.ShapeDtypeStruct((D_KV_HEADS, nt, LANES, tm), BF16),
        jax.ShapeDtypeStruct((S, 256), BF16),
    ]
    out_specs = [
        row(256), row(256), row(256), row(768), row(768), row(768),
        row(512), pl.BlockSpec((C_HEADS, 1, LANES, tm), lambda i: (0, i, 0, 0)), row(512),
        row(512), pl.BlockSpec((D_KV_HEADS, 1, LANES, tm), lambda i: (0, i, 0, 0)), row(256),
    ]
    return pl.pallas_call(
        _proj_kernel, grid=(nt,), in_specs=in_specs, out_specs=out_specs, out_shape=out_shape,
        compiler_params=_cparams(("parallel",)), name="proj",
    )(h, lw["g1"], lw["w1"], lw["gains"], consts["bd64"], consts["bd128"], consts["bdc"],
      consts["bd256"], consts["bdkv"], lw["wuq"], lw["wukv"], lw["gcq"], lw["gckn"], tabc, tabd)


def _banded_kernel(*refs, seq_len, has_sink):
    if has_sink:
        sink_ref, refs = refs[0], refs[1:]
    q_ref, kp_ref, kc_ref, kn_ref, vp_ref, vc_ref, vn_ref, bias_ref, o_ref, lse_ref = refs
    i = pl.program_id(1)
    q = q_ref[...]
    k = jnp.concatenate([kp_ref[...], kc_ref[...], kn_ref[...]], axis=0)
    v = jnp.concatenate([vp_ref[...], vc_ref[...], vn_ref[...]], axis=0)
    kpos = (i - 1) * QBLK + lax.broadcasted_iota(jnp.int32, (1, 3 * QBLK), 1)
    valid = (kpos >= 0) & (kpos < seq_len)
    lane = lax.broadcasted_iota(jnp.int32, (1, 4 * HEAD_DIM), 1)
    out = jnp.zeros((QBLK, 4 * HEAD_DIM), F32)
    lse = jnp.zeros((QBLK, 4 * HEAD_DIM), F32)
    for h in range(4):
        hmask = (lane >= h * HEAD_DIM) & (lane < (h + 1) * HEAD_DIM)
        qh = jnp.where(hmask, q, jnp.zeros_like(q))
        s = lax.dot_general(qh, k, (((1,), (1,)), ((), ())), preferred_element_type=F32)
        s = jnp.where(valid, s + bias_ref[h], NEG_INF)
        m = jnp.max(s, axis=-1, keepdims=True)
        if has_sink:
            m = jnp.maximum(m, sink_ref[h])
        p = jnp.exp(s - m)
        denom = jnp.sum(p, axis=-1, keepdims=True)
        if has_sink:
            denom = denom + jnp.exp(sink_ref[h] - m)
        pv = jnp.dot(p.astype(BF16), v, preferred_element_type=F32)
        out = out + jnp.where(hmask, pv / denom, 0.0)
        lse = lse + jnp.where(hmask, m + jnp.log(denom), 0.0)
    o_ref[...] = out.astype(o_ref.dtype)
    lse_ref[...] = lse


def _banded_call(q, k, v, bias, sink, dil):
    L = q.shape[0]
    nb = L // QBLK
    blk = lambda f: pl.BlockSpec((QBLK, 256), f)
    cur = lambda r, i: (i, r)
    prev = lambda r, i: (jnp.maximum(i - 1, 0), r)
    nxt = lambda r, i: (jnp.minimum(i + 1, nb - 1), r)
    in_specs = [blk(cur), blk(prev), blk(cur), blk(nxt), blk(prev), blk(cur), blk(nxt),
                _full((4, QBLK, 3 * QBLK))]
    args = [q, k, k, k, v, v, v, bias]
    if sink is not None:
        in_specs = [pl.BlockSpec(memory_space=pltpu.SMEM)] + in_specs
        args = [sink] + args
    return pl.pallas_call(
        functools.partial(_banded_kernel, seq_len=L, has_sink=sink is not None),
        grid=(dil, nb), in_specs=in_specs,
        out_specs=[blk(cur), blk(cur)],
        out_shape=[jax.ShapeDtypeStruct(q.shape, BF16), jax.ShapeDtypeStruct(q.shape, F32)],
        compiler_params=_cparams(("parallel", "parallel")), name="banded",
    )(*args)


def _flash_kernel(q_ref, kt_ref, v_ref, o_ref, s_buf, p_buf, a_buf, m_ref, acc_ref, *, n_chunks, chunk):
    per = FLASH_CHUNKS
    n_steps = n_chunks // per
    width = per * chunk

    def scores(t, slot):
        kt = jnp.concatenate([kt_ref[t * per + j] for j in range(per)], axis=1)
        s_buf[slot] = jnp.dot(q_ref[...], kt, preferred_element_type=F32)

    def softmax(slot):
        s = s_buf[slot]
        m = m_ref[...]
        m_new = jnp.maximum(m, jnp.max(s, axis=-1, keepdims=True))
        m_ref[...] = m_new
        a_buf[slot] = jnp.exp2(m - m_new)
        p_buf[slot] = jnp.exp2(s - m_new).astype(BF16)

    def accumulate(t, slot):
        vv = v_ref[pl.ds(pl.multiple_of(t * width, width), width), :]
        acc_ref[...] = a_buf[slot] * acc_ref[...] + jnp.dot(p_buf[slot], vv, preferred_element_type=F32)

    m_ref[...] = jnp.full(m_ref.shape, -jnp.inf, F32)
    acc_ref[...] = jnp.zeros(acc_ref.shape, F32)
    scores(0, 0)
    scores(1, 1)
    softmax(0)

    def body(i, carry):
        t = 2 * i
        scores(t, 0)
        softmax(1)
        accumulate(t - 2, 0)
        scores(t + 1, 1)
        softmax(0)
        accumulate(t - 1, 1)
        return carry

    lax.fori_loop(1, n_steps // 2, body, 0)
    softmax(1)
    accumulate(n_steps - 2, 0)
    accumulate(n_steps - 1, 1)
    acc = acc_ref[...]
    lane = lax.broadcasted_iota(jnp.int32, (1, LANES), 1)
    o_ref[...] = jnp.where(lane < HEAD_DIM, acc / acc[:, HEAD_DIM:HEAD_DIM + 1], 0.0).astype(o_ref.dtype)


def _flash_call(q, kt, v, n_heads, group):
    S = q.shape[0]
    n_chunks, chunk = kt.shape[1], kt.shape[3]
    tq = min(TQ_FLASH, S)
    width = FLASH_CHUNKS * chunk
    assert n_chunks % (2 * FLASH_CHUNKS) == 0 and n_chunks >= 4 * FLASH_CHUNKS
    return pl.pallas_call(
        functools.partial(_flash_kernel, n_chunks=n_chunks, chunk=chunk),
        grid=(n_heads, S // tq),
        in_specs=[pl.BlockSpec((tq, LANES), lambda h, i: (i, h)),
                  pl.BlockSpec((None, n_chunks, LANES, chunk), lambda h, i: (h // group, 0, 0, 0)),
                  pl.BlockSpec((S, LANES), lambda h, i: (0, h // group))],
        out_specs=pl.BlockSpec((tq, LANES), lambda h, i: (i, h)),
        out_shape=jax.ShapeDtypeStruct((S, n_heads * LANES), BF16),
        scratch_shapes=[pltpu.VMEM((2, tq, width), F32), pltpu.VMEM((2, tq, width), BF16),
                        pltpu.VMEM((2, tq, 1), F32), pltpu.VMEM((tq, 1), F32), pltpu.VMEM((tq, LANES), F32)],
        compiler_params=_cparams(("parallel", "parallel")), name="flash",
    )(q, kt, v)


def _merge_kernel(h_ref, g1_ref, wg_ref, gb_ref, ya_ref, ob1_ref, ob2_ref, ob3_ref,
                  ls1_ref, ls2_ref, ls3_ref, yc_ref, yd_ref,
                  wba_ref, wbb_ref, wbc_ref, wbd_ref, wo_ref, o_ref):
    x = h_ref[...]
    ms = jnp.mean(x * x, axis=-1, keepdims=True)
    hn = (x * lax.rsqrt(ms + EPS) * g1_ref[...]).astype(BF16)

    l1, l2, l3 = ls1_ref[...], ls2_ref[...], ls3_ref[...]
    mx = jnp.maximum(jnp.maximum(l1, l2), l3)
    e1, e2, e3 = jnp.exp(l1 - mx), jnp.exp(l2 - mx), jnp.exp(l3 - mx)
    yb = (e1 * ob1_ref[...].astype(F32) + e2 * ob2_ref[...].astype(F32)
          + e3 * ob3_ref[...].astype(F32)) / (e1 + e2 + e3)

    ys = (ya_ref[...], yb.astype(BF16), yc_ref[...], yd_ref[...])
    wbs = (wba_ref, wbb_ref, wbc_ref, wbd_ref)
    merged = jnp.zeros((x.shape[0], D_MODEL), F32)
    for n in range(N_BRANCHES):
        o = n * D_MODEL
        gpre = jnp.dot(hn, wg_ref[:, o:o + D_MODEL], preferred_element_type=F32) + gb_ref[:, o:o + D_MODEL]
        br = jnp.dot(ys[n], wbs[n][...], preferred_element_type=F32)
        merged = merged + jax.nn.sigmoid(gpre) * br
    o_ref[...] = x + jnp.dot(merged.astype(BF16), wo_ref[...], preferred_element_type=F32)


def _merge_call(h, lw, ya, obs, lses, yc, yd):
    S = h.shape[0]
    tm = TM_MERGE
    row = lambda w: pl.BlockSpec((tm, w), lambda i: (i, 0))
    in_specs = [row(D_MODEL), _full((1, D_MODEL)), _full((D_MODEL, 4 * D_MODEL)), _full((1, 4 * D_MODEL)),
                row(256), row(256), row(256), row(256), row(256), row(256), row(256), row(512), row(512),
                _full((256, D_MODEL)), _full((256, D_MODEL)), _full((512, D_MODEL)), _full((512, D_MODEL)),
                _full((D_MODEL, D_MODEL))]
    return pl.pallas_call(
        _merge_kernel, grid=(S // tm,), in_specs=in_specs, out_specs=row(D_MODEL),
        out_shape=jax.ShapeDtypeStruct((S, D_MODEL), F32),
        compiler_params=_cparams(("parallel",)), name="merge",
    )(h, lw["g1"], lw["wg"], lw["gb"], ya, obs[0], obs[1], obs[2], lses[0], lses[1], lses[2], yc, yd,
      lw["wba"], lw["wbb"], lw["wbc"], lw["wbd"], lw["wo"])


def _top16_desc(s):
    vals = []
    cur = s
    for _ in range(PEER_TOPK):
        m = jnp.max(cur, axis=0, keepdims=True)
        vals.append(m)
        cur = jnp.where(cur == m, -jnp.inf, cur)
    return vals


def _stack_rows(rows):
    n = len(rows)
    ridx = lax.broadcasted_iota(jnp.int32, (n, rows[0].shape[1]), 0)
    out = jnp.zeros((n, rows[0].shape[1]), F32)
    for r, v in enumerate(rows):
        out = jnp.where(ridx == r, v, out)
    return out


def _peerprep_kernel(h_ref, g2_ref, wqt_ref, keys_ref, xt_ref, th_ref, e1_ref, s2_ref, e2_ref):
    x = h_ref[...]
    ms = jnp.mean(x * x, axis=-1, keepdims=True)
    xn = x * lax.rsqrt(ms + EPS) * g2_ref[...]
    xt = xn.T.astype(BF16)
    xt_ref[...] = xt
    qt = jnp.dot(wqt_ref[...], xt, preferred_element_type=F32).astype(BF16)
    for h in range(PEER_HEADS):
        o = h * PEER_KEY_DIM
        s1 = jnp.dot(keys_ref[h, 0], qt[o:o + PEER_HALF], preferred_element_type=F32)
        s2 = jnp.dot(keys_ref[h, 1], qt[o + PEER_HALF:o + PEER_KEY_DIM], preferred_element_type=F32)
        v1 = _top16_desc(s1)
        v2 = _stack_rows(_top16_desc(s2))
        cand = jnp.concatenate([v1[0] + v2] + [v1[a] + v2[0:8] for a in range(1, PEER_TOPK)], axis=0)
        top = _top16_desc(cand)
        t0, tau = top[0], top[PEER_TOPK - 1]
        z = top[0] - t0
        z = jnp.exp(z)
        for t in top[1:]:
            z = z + jnp.exp(t - t0)
        v20 = v2[0:1]
        th = tau - s1
        e1 = jnp.exp(s1 - v1[0]) / z
        for c in range(x.shape[0] // LANES):
            th_ref[h, c] = th[:, c * LANES:(c + 1) * LANES]
            e1_ref[h, c] = e1[:, c * LANES:(c + 1) * LANES]
        s2_ref[h] = s2
        e2_ref[h] = jnp.exp(s2 - v20)


def _peerprep_call(h, lw):
    S = h.shape[0]
    tm = min(TM_PREP, S)
    hs = pl.BlockSpec((PEER_HEADS, PEER_N_KEYS, tm), lambda i: (0, 0, i))
    hshape = jax.ShapeDtypeStruct((PEER_HEADS, PEER_N_KEYS, S), F32)
    cs = pl.BlockSpec((PEER_HEADS, tm // LANES, PEER_N_KEYS, LANES), lambda i: (0, i, 0, 0))
    cshape = jax.ShapeDtypeStruct((PEER_HEADS, S // LANES, PEER_N_KEYS, LANES), F32)
    return pl.pallas_call(
        _peerprep_kernel, grid=(S // tm,),
        in_specs=[pl.BlockSpec((tm, D_MODEL), lambda i: (i, 0)), _full((1, D_MODEL)),
                  _full((D_MODEL, D_MODEL)), _full((PEER_HEADS, 2, PEER_N_KEYS, PEER_HALF))],
        out_specs=[pl.BlockSpec((D_MODEL, tm), lambda i: (0, i)), cs, cs, hs, hs],
        out_shape=[jax.ShapeDtypeStruct((D_MODEL, S), BF16), cshape, cshape, hshape, hshape],
        compiler_params=_cparams(("parallel",)), name="peerprep",
    )(h, lw["g2"], lw["wqt"], lw["keys"])


def _replicated_row(ref, h, c, row, n):
    return jnp.concatenate([ref[h, c, pl.ds(row, 8, stride=0), :]] * (n // 8), axis=0)


def _peer_kernel(h_ref, xt_ref, wu_ref, wvt_ref, th_ref, e1_ref, s2_ref, e2_ref, o_ref, acc_ref, a_ref, ag_ref):
    j, e = pl.program_id(0), pl.program_id(1)

    @pl.when((j == 0) & (e == 0))
    def _():
        ag_ref[...] = jnp.zeros_like(ag_ref)
        acc_ref[...] = jnp.zeros_like(acc_ref)

    contrib = jnp.dot(wvt_ref[...], ag_ref[...], preferred_element_type=F32)
    acc_ref[...] = jnp.where(e == 0, 0.0, acc_ref[...] + contrib)

    a_ref[...] = jnp.dot(wu_ref[...], xt_ref[...], preferred_element_type=F32)
    rt = PEER_ROW_TILE
    for il in range(ET_PEER // PEER_N_KEYS):
        for c in range(a_ref.shape[1] // LANES):
            cl = slice(c * LANES, (c + 1) * LANES)
            th = [_replicated_row(th_ref, h, c, il, rt) for h in range(PEER_HEADS)]
            w1 = [_replicated_row(e1_ref, h, c, il, rt) for h in range(PEER_HEADS)]
            for r in range(0, PEER_N_KEYS, rt):
                g = jnp.where(s2_ref[0, r:r + rt, cl] >= th[0], e2_ref[0, r:r + rt, cl] * w1[0], 0.0)
                for h in range(1, PEER_HEADS):
                    g = g + jnp.where(s2_ref[h, r:r + rt, cl] >= th[h], e2_ref[h, r:r + rt, cl] * w1[h], 0.0)
                row = il * PEER_N_KEYS + r
                a = a_ref[row:row + rt, cl]
                act = 0.5 * a * (1.0 + lax.erf(a * (1.0 / math.sqrt(2.0))))
                ag_ref[row:row + rt, cl] = (act * g).astype(BF16)

    @pl.when(e == pl.num_programs(1) - 1)
    def _():
        o_ref[...] = h_ref[...] + acc_ref[...].T


def _peer_call(h, xt, th, e1, s2, e2, lw):
    S = h.shape[0]
    tm = min(TM_PEER, S)
    n_exp = lw["wu"].shape[0]
    rows = ET_PEER // PEER_N_KEYS
    n_tiles = n_exp // ET_PEER
    cur = lambda e: jnp.minimum(e, n_tiles - 1)
    prev = lambda e: jnp.maximum(e - 1, 0)
    tile = pl.BlockSpec((PEER_HEADS, tm // LANES, rows, LANES), lambda j, e: (0, j, cur(e), 0))
    res = pl.BlockSpec((PEER_HEADS, PEER_N_KEYS, tm), lambda j, e: (0, 0, j))
    return pl.pallas_call(
        _peer_kernel, grid=(S // tm, n_tiles + 1),
        in_specs=[pl.BlockSpec((tm, D_MODEL), lambda j, e: (j, 0)),
                  pl.BlockSpec((D_MODEL, tm), lambda j, e: (0, j)),
                  pl.BlockSpec((ET_PEER, D_MODEL), lambda j, e: (cur(e), 0)),
                  pl.BlockSpec((D_MODEL, ET_PEER), lambda j, e: (0, prev(e))),
                  tile, tile, res, res],
        out_specs=pl.BlockSpec((tm, D_MODEL), lambda j, e: (j, 0)),
        out_shape=jax.ShapeDtypeStruct((S, D_MODEL), F32),
        scratch_shapes=[pltpu.VMEM((D_MODEL, tm), F32), pltpu.VMEM((ET_PEER, tm), F32),
                        pltpu.VMEM((ET_PEER, tm), BF16)],
        compiler_params=_cparams(("arbitrary", "arbitrary")), name="peer",
    )(h, xt, lw["wu"], lw["wvt"], th, e1, s2, e2)


def _pad_heads(w, n_heads, width):
    lead = w.shape[:-1]
    w = w.reshape(*lead, n_heads, width)
    w = jnp.pad(w, [(0, 0)] * len(lead) + [(0, 0), (0, LANES - width)])
    return w.reshape(*lead, n_heads * LANES)


def _dup_kv(w):
    lead = w.shape[:-1]
    w = w.reshape(*lead, A_KV_HEADS, 1, HEAD_DIM)
    w = jnp.broadcast_to(w, (*lead, A_KV_HEADS, A_HEADS // A_KV_HEADS, HEAD_DIM))
    return w.reshape(*lead, A_HEADS * HEAD_DIM)


def _block_diag(width, segments):
    m = np.zeros((width, width), np.float32)
    for a, b, d in segments:
        m[a:b, a:b] = 1.0 / d
    return jnp.asarray(m, BF16)


def _consts():
    seg64 = [(i * 64, (i + 1) * 64, 64) for i in range(4)]
    seg128 = [(i * 128, (i + 1) * 128, 64) for i in range(2)]
    segc = []
    for i in range(2):
        segc += [(i * 128, i * 128 + 64, 64), (i * 128 + 64, (i + 1) * 128, 32)]
    return {
        "bd64": _block_diag(256, seg64), "bd128": _block_diag(256, seg128), "bdc": _block_diag(256, segc),
        "bd256": _block_diag(256, [(0, 256, 256)]), "bdkv": _block_diag(128, [(0, 128, 128)]),
    }


def _layer_weights(l, p):
    w = p["w_in"][l]
    cols = [w[:, IN_OFFS[i]:IN_OFFS[i + 1]] for i in range(len(IN_SPLITS))]
    aq, ak, av, bq, bk, bv, cq, ckv, ckr, dq, dk, dv, wgate = cols
    ckr_rep = jnp.zeros((D_MODEL, C_HEADS, LANES), F32).at[:, :, C_NOPE:C_QK_DIM].set(ckr[:, None, :])
    w1 = jnp.concatenate([
        aq, _dup_kv(ak), _dup_kv(av), bq, bk, bv, cq, ckv, ckr_rep.reshape(D_MODEL, C_HEADS * LANES),
        _pad_heads(dq, D_HEADS, HEAD_DIM), _pad_heads(dk, D_KV_HEADS, HEAD_DIM),
        _pad_heads(dv, D_KV_HEADS, HEAD_DIM)], axis=1).astype(BF16)

    scale = HEAD_DIM ** -0.5
    ones = lambda n: jnp.ones((n,), F32)
    ones_col = jnp.zeros((2, LANES), F32).at[:, HEAD_DIM].set(1.0).reshape(-1)
    ck_gain = p["c_k_gain"][l]
    ckr_gain = jnp.zeros((C_HEADS, LANES), F32).at[:, C_NOPE:C_QK_DIM].set(ck_gain[None, C_NOPE:])
    gains = jnp.concatenate([
        jnp.tile(p["a_q_gain"][l], 4) * scale, jnp.tile(p["a_k_gain"][l], 4), ones(256),
        jnp.tile(p["b_q_gain"][l], (1, 4)).reshape(-1) * scale, jnp.tile(p["b_k_gain"][l], (1, 4)).reshape(-1),
        ones(768),
        p["c_q_lat_gain"][l], p["c_kv_lat_gain"][l], ckr_gain.reshape(-1),
        jnp.tile(_pad_heads(p["d_q_gain"][l], 1, HEAD_DIM), 4) * (scale * LOG2E),
        jnp.tile(_pad_heads(p["d_k_gain"][l], 1, HEAD_DIM), 2), ones_col])[None, :]

    cq_gain = jnp.tile(_pad_heads(p["c_q_gain"][l], 1, C_QK_DIM), C_HEADS) * (C_QK_DIM ** -0.5 * LOG2E)
    ckn_gain = jnp.tile(_pad_heads(ck_gain[:C_NOPE], 1, C_NOPE), C_HEADS)
    wukv = p["c_w_ukv"][l].reshape(C_KV_RANK, C_HEADS, C_NOPE + C_V)
    wukv = jnp.concatenate([_pad_heads(wukv[:, :, :C_NOPE].reshape(C_KV_RANK, -1), C_HEADS, C_NOPE),
                            _pad_heads(wukv[:, :, C_NOPE:].reshape(C_KV_RANK, -1), C_HEADS, C_V)], axis=1)

    wb = p["w_branch"][l]
    pad_rows = lambda m: _pad_heads(m.T, 4, HEAD_DIM).T
    return {
        "g1": p["norm1_gain"][l][None, :], "w1": w1, "gains": gains,
        "wuq": _pad_heads(p["c_w_uq"][l], C_HEADS, C_QK_DIM).astype(BF16), "wukv": wukv.astype(BF16),
        "gcq": cq_gain[None, :], "gckn": ckn_gain[None, :],
        "wg": wgate.astype(BF16), "gb": p["gate_bias"][l][None, :],
        "wba": wb[0].astype(BF16), "wbb": wb[1].astype(BF16),
        "wbc": pad_rows(wb[2]).astype(BF16), "wbd": pad_rows(wb[3]).astype(BF16),
        "wo": p["w_out"][l].astype(BF16),
        "g2": p["norm2_gain"][l][None, :], "wqt": p["peer_w_q"][l].T.astype(BF16),
        "keys": p["peer_sub_keys"][l].astype(BF16),
        "wu": p["peer_w_u"][l].astype(BF16), "wvt": p["peer_w_v"][l].T.astype(BF16),
    }


def _rope_tables(S):
    half = 16
    freqs = ROPE_THETA ** (-jnp.arange(half, dtype=F32) / half)

    def cs(pos):
        ang = pos.astype(F32)[:, None] * freqs[None, :]
        return jnp.cos(ang), jnp.sin(ang)

    def table(parts):
        cos, sa, sb = [], [], []
        at = 0
        for start, (c, s) in parts:
            z = jnp.zeros((S, start - at), F32)
            zh = jnp.zeros((S, half), F32)
            cos += [z + 1.0, c, c]
            sa += [z, -s, zh]
            sb += [z, zh, s]
            at = start + 2 * half
        z = jnp.zeros((S, LANES - at), F32)
        cat = lambda pieces: jnp.concatenate(pieces, axis=1)
        return jnp.stack([cat(cos + [z + 1.0]), cat(sa + [z]), cat(sb + [z])])

    pos = jnp.arange(S, dtype=jnp.int32)
    tabc = table([(C_NOPE, cs(pos))])
    tabd = table([(0, cs(pos // GRID_W)), (2 * half, cs(pos % GRID_W))])
    return tabc, tabd


def _rel_bucket(rel):
    nb = REL_BUCKETS // 2
    max_exact = nb // 2
    ret = jnp.where(rel > 0, nb, 0)
    n = jnp.abs(rel)
    nf = jnp.maximum(n, 1).astype(F32)
    large = max_exact + (jnp.log(nf / max_exact) / math.log(REL_MAX_DIST / max_exact)
                         * (nb - max_exact)).astype(jnp.int32)
    large = jnp.minimum(large, nb - 1)
    return ret + jnp.where(n < max_exact, n, large)


def _band_bias(rel_table, half, dil):
    rel = jnp.arange(3 * QBLK)[None, :] - QBLK - jnp.arange(QBLK)[:, None]
    onehot = (_rel_bucket(rel * dil)[None, :, :] == jnp.arange(REL_BUCKETS)[:, None, None]).astype(F32)
    bias = jnp.einsum("bh,bqs->hqs", rel_table.astype(F32), onehot, precision=lax.Precision.HIGHEST)
    return jnp.where((jnp.abs(rel) <= half)[None], bias, NEG_INF)


def kernel(x, rel_bias, norm1_gain, w_in, gate_bias, a_q_gain, a_k_gain, a_sink, b_q_gain, b_k_gain, c_q_lat_gain, c_kv_lat_gain, c_w_uq, c_w_ukv, c_q_gain, c_k_gain, d_q_gain, d_k_gain, w_branch, w_out, norm2_gain, peer_w_q, peer_sub_keys, peer_w_u, peer_w_v):
    p = dict(rel_bias=rel_bias, norm1_gain=norm1_gain, w_in=w_in, gate_bias=gate_bias, a_q_gain=a_q_gain,
             a_k_gain=a_k_gain, a_sink=a_sink, b_q_gain=b_q_gain, b_k_gain=b_k_gain,
             c_q_lat_gain=c_q_lat_gain, c_kv_lat_gain=c_kv_lat_gain, c_w_uq=c_w_uq, c_w_ukv=c_w_ukv,
             c_q_gain=c_q_gain, c_k_gain=c_k_gain, d_q_gain=d_q_gain, d_k_gain=d_k_gain, w_branch=w_branch,
             w_out=w_out, norm2_gain=norm2_gain, peer_w_q=peer_w_q, peer_sub_keys=peer_sub_keys,
             peer_w_u=peer_w_u, peer_w_v=peer_w_v)
    Bn, S, D = x.shape
    assert Bn == 1 and D == D_MODEL and S % (16 * QBLK) == 0
    depth = w_in.shape[0]
    consts = _consts()
    tabc, tabd = _rope_tables(S)
    bias_a = _band_bias(rel_bias[:, :A_HEADS], A_HALF_WINDOW, 1)
    bias_b = [_band_bias(rel_bias[:, A_HEADS + g * B_HEADS:A_HEADS + (g + 1) * B_HEADS], win // (2 * dil), dil)
              for g, (win, dil) in enumerate(B_PATTERNS)]

    h = x.reshape(S, D)
    for l in range(depth):
        lw = _layer_weights(l, p)
        qa, ka, va, qb, kb, vb, qc, kct, vc, qd, kdt, vd = _proj_call(h, lw, consts, tabc, tabd)
        ya, _ = _banded_call(qa, ka, va, bias_a, a_sink[l], 1)
        obs, lses = [], []
        for g, (win, dil) in enumerate(B_PATTERNS):
            view = lambda t: t[:, g * 256:(g + 1) * 256].reshape(S // dil, dil * 256)
            o, lse = _banded_call(view(qb), view(kb), view(vb), bias_b[g], None, dil)
            obs.append(o.reshape(S, 256))
            lses.append(lse.reshape(S, 256))
        yc = _flash_call(qc, kct, vc, C_HEADS, 1)
        yd = _flash_call(qd, kdt, vd, D_HEADS, D_HEADS // D_KV_HEADS)
        h = _merge_call(h, lw, ya, obs, lses, yc, yd)
        xt, th, e1, s2, e2 = _peerprep_call(h, lw)
        h = _peer_call(h, xt, th, e1, s2, e2, lw)
    return h.reshape(Bn, S, D)
```

```python
import functools
import math

import jax
import jax.numpy as jnp
import numpy as np
from jax import lax
from jax.experimental import pallas as pl
from jax.experimental.pallas import tpu as pltpu

F32 = jnp.float32
BF16 = jnp.bfloat16

D_MODEL = 1024
HEAD_DIM = 64
GRID_W = 64
EPS = 1e-6
NEG_INF = -1e30
LOG2E = math.log2(math.e)
ROPE_THETA = 10000.0

A_HEADS, A_KV_HEADS, A_HALF_WINDOW = 4, 2, 128
B_PATTERNS = ((128, 1), (512, 4), (2048, 16))
B_GROUPS, B_HEADS = 3, 4
C_HEADS, C_Q_RANK, C_KV_RANK, C_NOPE, C_ROPE, C_V = 4, 256, 128, 64, 32, 64
C_QK_DIM = C_NOPE + C_ROPE
D_HEADS, D_KV_HEADS = 4, 2
N_BRANCHES = 4
REL_BUCKETS, REL_MAX_DIST = 32, 1024
PEER_HEADS, PEER_N_KEYS, PEER_KEY_DIM, PEER_TOPK = 8, 128, 128, 16
PEER_HALF = PEER_KEY_DIM // 2

IN_SPLITS = (256, 128, 128, 768, 768, 768, C_Q_RANK, C_KV_RANK, C_ROPE, 256, 128, 128, N_BRANCHES * D_MODEL)
IN_OFFS = tuple(int(o) for o in np.cumsum((0,) + IN_SPLITS))

LANES = 128
QBLK = 128
VMEM_LIMIT = 56 * 1024 * 1024

P_AQ, P_AK, P_AV = 0, 256, 512
P_BQ, P_BK, P_BV = 768, 1536, 2304
P_CQ, P_CKV, P_CKR = 3072, 3328, 3456
P_DQ, P_DK, P_DV = 3968, 4480, 4736
P_WIDTH = 4992

TM_PROJ = 256
TM_MERGE = 256
TQ_FLASH = 512
FLASH_CHUNKS = 4
TM_PREP = 256
TM_PEER = 512
PEER_ROW_TILE = 16
ET_PEER = 1024


def _cparams(sem):
    return pltpu.CompilerParams(dimension_semantics=sem, vmem_limit_bytes=VMEM_LIMIT)


def _full(shape):
    n = len(shape)
    return pl.BlockSpec(shape, lambda *_: (0,) * n)


def _seg_norm(t, bd, gain):
    ms = jnp.dot((t * t).astype(BF16), bd, preferred_element_type=F32)
    return t * lax.rsqrt(ms + EPS) * gain


def _rope128(x, tab_ref):
    up = pltpu.roll(x, LANES - 16, axis=1)
    dn = pltpu.roll(x, 16, axis=1)
    return x * tab_ref[0] + up * tab_ref[1] + dn * tab_ref[2]


def _proj_kernel(h_ref, g1_ref, w_ref, gains_ref, bd64_ref, bd128_ref, bdc_ref, bd256_ref, bdkv_ref,
                 wuq_ref, wukv_ref, gcq_ref, gckn_ref, tabc_ref, tabd_ref,
                 qa_ref, ka_ref, va_ref, qb_ref, kb_ref, vb_ref,
                 qc_ref, kct_ref, vc_ref, qd_ref, kdt_ref, vd_ref):
    x = h_ref[...]
    ms = jnp.mean(x * x, axis=-1, keepdims=True)
    hn = (x * lax.rsqrt(ms + EPS) * g1_ref[...]).astype(BF16)

    def proj(off, width):
        return jnp.dot(hn, w_ref[:, off:off + width], preferred_element_type=F32)

    def gain(off, width):
        return gains_ref[:, off:off + width]

    bd64 = bd64_ref[...]
    bd128 = bd128_ref[...]
    bdc = bdc_ref[...]
    ones_col = gain(P_DV, 256)

    qa_ref[...] = _seg_norm(proj(P_AQ, 256), bd64, gain(P_AQ, 256)).astype(BF16)
    ka_ref[...] = _seg_norm(proj(P_AK, 256), bd64, gain(P_AK, 256)).astype(BF16)
    va_ref[...] = proj(P_AV, 256).astype(BF16)

    for g in range(B_GROUPS):
        o = g * 256
        qb_ref[:, o:o + 256] = _seg_norm(proj(P_BQ + o, 256), bd64, gain(P_BQ + o, 256)).astype(BF16)
        kb_ref[:, o:o + 256] = _seg_norm(proj(P_BK + o, 256), bd64, gain(P_BK + o, 256)).astype(BF16)
        vb_ref[:, o:o + 256] = proj(P_BV + o, 256).astype(BF16)

    cq = _seg_norm(proj(P_CQ, 256), bd256_ref[...], gain(P_CQ, 256)).astype(BF16)
    ckv = _seg_norm(proj(P_CKV, 128), bdkv_ref[...], gain(P_CKV, 128)).astype(BF16)
    for hb in range(2):
        o = hb * 256
        q = jnp.dot(cq, wuq_ref[:, o:o + 256], preferred_element_type=F32)
        q = _seg_norm(q, bdc, gcq_ref[:, o:o + 256])
        kn = jnp.dot(ckv, wukv_ref[:, o:o + 256], preferred_element_type=F32)
        kn = _seg_norm(kn, bd128, gckn_ref[:, o:o + 256])
        kp = _seg_norm(proj(P_CKR + o, 256), bdc, gain(P_CKR + o, 256))
        vc_ref[:, o:o + 256] = (jnp.dot(ckv, wukv_ref[:, 512 + o:512 + o + 256], preferred_element_type=F32)
                                + ones_col).astype(BF16)
        for j in range(2):
            c = j * LANES
            hd = hb * 2 + j
            qc_ref[:, o + c:o + c + LANES] = _rope128(q[:, c:c + LANES], tabc_ref).astype(BF16)
            k = kn[:, c:c + LANES] + _rope128(kp[:, c:c + LANES], tabc_ref)
            kct_ref[hd, 0] = k.T.astype(BF16)

    for hb in range(2):
        o = hb * 256
        q = _seg_norm(proj(P_DQ + o, 256), bd128, gain(P_DQ + o, 256))
        for j in range(2):
            c = j * LANES
            qd_ref[:, o + c:o + c + LANES] = _rope128(q[:, c:c + LANES], tabd_ref).astype(BF16)
    k = _seg_norm(proj(P_DK, 256), bd128, gain(P_DK, 256))
    for j in range(D_KV_HEADS):
        c = j * LANES
        kdt_ref[j, 0] = _rope128(k[:, c:c + LANES], tabd_ref).T.astype(BF16)
    vd_ref[...] = (proj(P_DV, 256) + ones_col).astype(BF16)


def _proj_call(h, lw, consts, tabc, tabd):
    S = h.shape[0]
    tm = TM_PROJ
    nt = S // tm
    row = lambda w: pl.BlockSpec((tm, w), lambda i: (i, 0))
    in_specs = [
        row(D_MODEL), _full((1, D_MODEL)), _full((D_MODEL, P_WIDTH)), _full((1, P_WIDTH)),
        _full((256, 256)), _full((256, 256)), _full((256, 256)), _full((256, 256)), _full((128, 128)),
        _full((C_Q_RANK, 512)), _full((C_KV_RANK, 1024)), _full((1, 512)), _full((1, 512)),
        pl.BlockSpec((3, tm, LANES), lambda i: (0, i, 0)),
        pl.BlockSpec((3, tm, LANES), lambda i: (0, i, 0)),
    ]
    out_shape = [
        jax.ShapeDtypeStruct((S, 256), BF16), jax.ShapeDtypeStruct((S, 256), BF16),
        jax.ShapeDtypeStruct((S, 256), BF16),
        jax.ShapeDtypeStruct((S, 768), BF16), jax.ShapeDtypeStruct((S, 768), BF16),
        jax.ShapeDtypeStruct((S, 768), BF16),
        jax.ShapeDtypeStruct((S, 512), BF16), jax.ShapeDtypeStruct((C_HEADS, nt, LANES, tm), BF16),
        jax.ShapeDtypeStruct((S, 512), BF16),
        jax.ShapeDtypeStruct((S, 512), BF16), jax.ShapeDtypeStruct((D_KV_HEADS, nt, LANES, tm), BF16),
        jax.ShapeDtypeStruct((S, 256), BF16),
    ]
    out_specs = [
        row(256), row(256), row(256), row(768), row(768), row(768),
        row(512), pl.BlockSpec((C_HEADS, 1, LANES, tm), lambda i: (0, i, 0, 0)), row(512),
        row(512), pl.BlockSpec((D_KV_HEADS, 1, LANES, tm), lambda i: (0, i, 0, 0)), row(256),
    ]
    return pl.pallas_call(
        _proj_kernel, grid=(nt,), in_specs=in_specs, out_specs=out_specs, out_shape=out_shape,
        compiler_params=_cparams(("parallel",)), name="proj",
    )(h, lw["g1"], lw["w1"], lw["gains"], consts["bd64"], consts["bd128"], consts["bdc"],
      consts["bd256"], consts["bdkv"], lw["wuq"], lw["wukv"], lw["gcq"], lw["gckn"], tabc, tabd)


def _banded_kernel(*refs, seq_len, has_sink):
    if has_sink:
        sink_ref, refs = refs[0], refs[1:]
    q_ref, kp_ref, kc_ref, kn_ref, vp_ref, vc_ref, vn_ref, bias_ref, o_ref, lse_ref = refs
    i = pl.program_id(1)
    q = q_ref[...]
    k = jnp.concatenate([kp_ref[...], kc_ref[...], kn_ref[...]], axis=0)
    v = jnp.concatenate([vp_ref[...], vc_ref[...], vn_ref[...]], axis=0)
    kpos = (i - 1) * QBLK + lax.broadcasted_iota(jnp.int32, (1, 3 * QBLK), 1)
    valid = (kpos >= 0) & (kpos < seq_len)
    lane = lax.broadcasted_iota(jnp.int32, (1, 4 * HEAD_DIM), 1)
    out = jnp.zeros((QBLK, 4 * HEAD_DIM), F32)
    lse = jnp.zeros((QBLK, 4 * HEAD_DIM), F32)
    for h in range(4):
        hmask = (lane >= h * HEAD_DIM) & (lane < (h + 1) * HEAD_DIM)
        qh = jnp.where(hmask, q, jnp.zeros_like(q))
        s = lax.dot_general(qh, k, (((1,), (1,)), ((), ())), preferred_element_type=F32)
        s = jnp.where(valid, s + bias_ref[h], NEG_INF)
        m = jnp.max(s, axis=-1, keepdims=True)
        if has_sink:
            m = jnp.maximum(m, sink_ref[h])
        p = jnp.exp(s - m)
        denom = jnp.sum(p, axis=-1, keepdims=True)
        if has_sink:
            denom = denom + jnp.exp(sink_ref[h] - m)
        pv = jnp.dot(p.astype(BF16), v, preferred_element_type=F32)
        out = out + jnp.where(hmask, pv / denom, 0.0)
        lse = lse + jnp.where(hmask, m + jnp.log(denom), 0.0)
    o_ref[...] = out.astype(o_ref.dtype)
    lse_ref[...] = lse


def _banded_call(q, k, v, bias, sink, dil):
    L = q.shape[0]
    nb = L // QBLK
    blk = lambda f: pl.BlockSpec((QBLK, 256), f)
    cur = lambda r, i: (i, r)
    prev = lambda r, i: (jnp.maximum(i - 1, 0), r)
    nxt = lambda r, i: (jnp.minimum(i + 1, nb - 1), r)
    in_specs = [blk(cur), blk(prev), blk(cur), blk(nxt), blk(prev), blk(cur), blk(nxt),
                _full((4, QBLK, 3 * QBLK))]
    args = [q, k, k, k, v, v, v, bias]
    if sink is not None:
        in_specs = [pl.BlockSpec(memory_space=pltpu.SMEM)] + in_specs
        args = [sink] + args
    return pl.pallas_call(
        functools.partial(_banded_kernel, seq_len=L, has_sink=sink is not None),
        grid=(dil, nb), in_specs=in_specs,
        out_specs=[blk(cur), blk(cur)],
        out_shape=[jax.ShapeDtypeStruct(q.shape, BF16), jax.ShapeDtypeStruct(q.shape, F32)],
        compiler_params=_cparams(("parallel", "parallel")), name="banded",
    )(*args)


def _flash_kernel(q_ref, kt_ref, v_ref, o_ref, s_buf, p_buf, a_buf, m_ref, acc_ref, *, n_chunks, chunk):
    per = FLASH_CHUNKS
    n_steps = n_chunks // per
    width = per * chunk

    def scores(t, slot):
        kt = jnp.concatenate([kt_ref[t * per + j] for j in range(per)], axis=1)
        s_buf[slot] = jnp.dot(q_ref[...], kt, preferred_element_type=F32)

    def softmax(slot):
        s = s_buf[slot]
        m = m_ref[...]
        m_new = jnp.maximum(m, jnp.max(s, axis=-1, keepdims=True))
        m_ref[...] = m_new
        a_buf[slot] = jnp.exp2(m - m_new)
        p_buf[slot] = jnp.exp2(s - m_new).astype(BF16)

    def accumulate(t, slot):
        vv = v_ref[pl.ds(pl.multiple_of(t * width, width), width), :]
        acc_ref[...] = a_buf[slot] * acc_ref[...] + jnp.dot(p_buf[slot], vv, preferred_element_type=F32)

    m_ref[...] = jnp.full(m_ref.shape, -jnp.inf, F32)
    acc_ref[...] = jnp.zeros(acc_ref.shape, F32)
    scores(0, 0)
    scores(1, 1)
    softmax(0)

    def body(i, carry):
        t = 2 * i
        scores(t, 0)
        softmax(1)
        accumulate(t - 2, 0)
        scores(t + 1, 1)
        softmax(0)
        accumulate(t - 1, 1)
        return carry

    lax.fori_loop(1, n_steps // 2, body, 0)
    softmax(1)
    accumulate(n_steps - 2, 0)
    accumulate(n_steps - 1, 1)
    acc = acc_ref[...]
    lane = lax.broadcasted_iota(jnp.int32, (1, LANES), 1)
    o_ref[...] = jnp.where(lane < HEAD_DIM, acc / acc[:, HEAD_DIM:HEAD_DIM + 1], 0.0).astype(o_ref.dtype)


def _flash_call(q, kt, v, n_heads, group):
    S = q.shape[0]
    n_chunks, chunk = kt.shape[1], kt.shape[3]
    tq = min(TQ_FLASH, S)
    width = FLASH_CHUNKS * chunk
    assert n_chunks % (2 * FLASH_CHUNKS) == 0 and n_chunks >= 4 * FLASH_CHUNKS
    return pl.pallas_call(
        functools.partial(_flash_kernel, n_chunks=n_chunks, chunk=chunk),
        grid=(n_heads, S // tq),
        in_specs=[pl.BlockSpec((tq, LANES), lambda h, i: (i, h)),
                  pl.BlockSpec((None, n_chunks, LANES, chunk), lambda h, i: (h // group, 0, 0, 0)),
                  pl.BlockSpec((S, LANES), lambda h, i: (0, h // group))],
        out_specs=pl.BlockSpec((tq, LANES), lambda h, i: (i, h)),
        out_shape=jax.ShapeDtypeStruct((S, n_heads * LANES), BF16),
        scratch_shapes=[pltpu.VMEM((2, tq, width), F32), pltpu.VMEM((2, tq, width), BF16),
                        pltpu.VMEM((2, tq, 1), F32), pltpu.VMEM((tq, 1), F32), pltpu.VMEM((tq, LANES), F32)],
        compiler_params=_cparams(("parallel", "parallel")), name="flash",
    )(q, kt, v)


def _merge_kernel(h_ref, g1_ref, wg_ref, gb_ref, ya_ref, ob1_ref, ob2_ref, ob3_ref,
                  ls1_ref, ls2_ref, ls3_ref, yc_ref, yd_ref,
                  wba_ref, wbb_ref, wbc_ref, wbd_ref, wo_ref, o_ref):
    x = h_ref[...]
    ms = jnp.mean(x * x, axis=-1, keepdims=True)
    hn = (x * lax.rsqrt(ms + EPS) * g1_ref[...]).astype(BF16)

    l1, l2, l3 = ls1_ref[...], ls2_ref[...], ls3_ref[...]
    mx = jnp.maximum(jnp.maximum(l1, l2), l3)
    e1, e2, e3 = jnp.exp(l1 - mx), jnp.exp(l2 - mx), jnp.exp(l3 - mx)
    yb = (e1 * ob1_ref[...].astype(F32) + e2 * ob2_ref[...].astype(F32)
          + e3 * ob3_ref[...].astype(F32)) / (e1 + e2 + e3)

    ys = (ya_ref[...], yb.astype(BF16), yc_ref[...], yd_ref[...])
    wbs = (wba_ref, wbb_ref, wbc_ref, wbd_ref)
    merged = jnp.zeros((x.shape[0], D_MODEL), F32)
    for n in range(N_BRANCHES):
        o = n * D_MODEL
        gpre = jnp.dot(hn, wg_ref[:, o:o + D_MODEL], preferred_element_type=F32) + gb_ref[:, o:o + D_MODEL]
        br = jnp.dot(ys[n], wbs[n][...], preferred_element_type=F32)
        merged = merged + jax.nn.sigmoid(gpre) * br
    o_ref[...] = x + jnp.dot(merged.astype(BF16), wo_ref[...], preferred_element_type=F32)


def _merge_call(h, lw, ya, obs, lses, yc, yd):
    S = h.shape[0]
    tm = TM_MERGE
    row = lambda w: pl.BlockSpec((tm, w), lambda i: (i, 0))
    in_specs = [row(D_MODEL), _full((1, D_MODEL)), _full((D_MODEL, 4 * D_MODEL)), _full((1, 4 * D_MODEL)),
                row(256), row(256), row(256), row(256), row(256), row(256), row(256), row(512), row(512),
                _full((256, D_MODEL)), _full((256, D_MODEL)), _full((512, D_MODEL)), _full((512, D_MODEL)),
                _full((D_MODEL, D_MODEL))]
    return pl.pallas_call(
        _merge_kernel, grid=(S // tm,), in_specs=in_specs, out_specs=row(D_MODEL),
        out_shape=jax.ShapeDtypeStruct((S, D_MODEL), F32),
        compiler_params=_cparams(("parallel",)), name="merge",
    )(h, lw["g1"], lw["wg"], lw["gb"], ya, obs[0], obs[1], obs[2], lses[0], lses[1], lses[2], yc, yd,
      lw["wba"], lw["wbb"], lw["wbc"], lw["wbd"], lw["wo"])


def _top16_desc(s, want_rank=False):
    vals = []
    cur = s
    rank = jnp.full(s.shape, float(PEER_TOPK), F32)
    for k in range(PEER_TOPK):
        m = jnp.max(cur, axis=0, keepdims=True)
        vals.append(m)
        hit = cur == m
        if want_rank:
            rank = jnp.where(hit, float(k), rank)
        cur = jnp.where(hit, -jnp.inf, cur)
    return (vals, rank) if want_rank else vals


def _stack_rows(rows):
    n = len(rows)
    ridx = lax.broadcasted_iota(jnp.int32, (n, rows[0].shape[1]), 0)
    out = jnp.zeros((n, rows[0].shape[1]), F32)
    for r, v in enumerate(rows):
        out = jnp.where(ridx == r, v, out)
    return out


def _dup_bf16(x):
    u = pltpu.bitcast(x.astype(BF16).astype(F32), jnp.uint32)
    return u | (u >> 16)


def _peerprep_kernel(h_ref, g2_ref, wqt_ref, keys_ref, xt_ref, cnt_ref, w1_ref, r2_ref, e2_ref):
    x = h_ref[...]
    ms = jnp.mean(x * x, axis=-1, keepdims=True)
    xn = x * lax.rsqrt(ms + EPS) * g2_ref[...]
    xt = xn.T.astype(BF16)
    xt_ref[...] = xt
    qt = jnp.dot(wqt_ref[...], xt, preferred_element_type=F32).astype(BF16)
    for h in range(PEER_HEADS):
        o = h * PEER_KEY_DIM
        s1 = jnp.dot(keys_ref[h, 0], qt[o:o + PEER_HALF], preferred_element_type=F32)
        s2 = jnp.dot(keys_ref[h, 1], qt[o + PEER_HALF:o + PEER_KEY_DIM], preferred_element_type=F32)
        v1 = _top16_desc(s1)
        v2l, rank2 = _top16_desc(s2, want_rank=True)
        v2 = _stack_rows(v2l)
        cand = jnp.concatenate([v1[0] + v2] + [v1[a] + v2[0:8] for a in range(1, PEER_TOPK)], axis=0)
        top = _top16_desc(cand)
        t0, tau = top[0], top[PEER_TOPK - 1]
        z = jnp.exp(top[0] - t0)
        for t in top[1:]:
            z = z + jnp.exp(t - t0)
        cnt = jnp.zeros(s1.shape, F32)
        for b in range(PEER_TOPK):
            cnt = cnt + jnp.where(s1 + v2l[b] >= tau, 1.0, 0.0)
        cnt = _dup_bf16(cnt)
        w1 = _dup_bf16(jnp.exp(s1 - v1[0]) / z)
        for c in range(x.shape[0] // LANES):
            cnt_ref[h, c] = cnt[:, c * LANES:(c + 1) * LANES]
            w1_ref[h, c] = w1[:, c * LANES:(c + 1) * LANES]
        r2_ref[h] = rank2.astype(BF16)
        e2_ref[h] = jnp.exp(s2 - v2l[0]).astype(BF16)


def _peerprep_call(h, lw):
    S = h.shape[0]
    tm = min(TM_PREP, S)
    hs = pl.BlockSpec((PEER_HEADS, PEER_N_KEYS, tm), lambda i: (0, 0, i))
    hshape = jax.ShapeDtypeStruct((PEER_HEADS, PEER_N_KEYS, S), BF16)
    cs = pl.BlockSpec((PEER_HEADS, tm // LANES, PEER_N_KEYS, LANES), lambda i: (0, i, 0, 0))
    cshape = jax.ShapeDtypeStruct((PEER_HEADS, S // LANES, PEER_N_KEYS, LANES), jnp.uint32)
    return pl.pallas_call(
        _peerprep_kernel, grid=(S // tm,),
        in_specs=[pl.BlockSpec((tm, D_MODEL), lambda i: (i, 0)), _full((1, D_MODEL)),
                  _full((D_MODEL, D_MODEL)), _full((PEER_HEADS, 2, PEER_N_KEYS, PEER_HALF))],
        out_specs=[pl.BlockSpec((D_MODEL, tm), lambda i: (0, i)), cs, cs, hs, hs],
        out_shape=[jax.ShapeDtypeStruct((D_MODEL, S), BF16), cshape, cshape, hshape, hshape],
        compiler_params=_cparams(("parallel",)), name="peerprep",
    )(h, lw["g2"], lw["wqt"], lw["keys"])


def _replicated_row(ref, h, c, row):
    return ref[h, c, pl.ds(row, 8, stride=0), :]


def _peer_kernel(h_ref, xt_ref, wu_ref, wvt_ref, cnt_ref, w1_ref, r2_in, e2_in, o_ref,
                 acc_ref, a_ref, ag_ref, r2_ref, e2_ref):
    j, e = pl.program_id(0), pl.program_id(1)

    @pl.when((j == 0) & (e == 0))
    def _():
        ag_ref[...] = jnp.zeros_like(ag_ref)
        acc_ref[...] = jnp.zeros_like(acc_ref)

    @pl.when(e == 0)
    def _():
        r2_ref[...] = r2_in[...]
        e2_ref[...] = e2_in[...]

    contrib = jnp.dot(wvt_ref[...], ag_ref[...], preferred_element_type=F32)
    acc_ref[...] = jnp.where(e == 0, 0.0, acc_ref[...] + contrib)

    a_ref[...] = jnp.dot(wu_ref[...], xt_ref[...], preferred_element_type=F32)
    rt = PEER_ROW_TILE
    zero = jnp.zeros((rt, LANES), BF16)
    for il in range(ET_PEER // PEER_N_KEYS):
        for c in range(a_ref.shape[1] // LANES):
            cl = slice(c * LANES, (c + 1) * LANES)
            cnt = [pltpu.bitcast(_replicated_row(cnt_ref, h, c, il), BF16) for h in range(PEER_HEADS)]
            w1 = [pltpu.bitcast(_replicated_row(w1_ref, h, c, il), BF16) for h in range(PEER_HEADS)]
            for r in range(0, PEER_N_KEYS, rt):
                hr = lambda h: slice(h * PEER_N_KEYS + r, h * PEER_N_KEYS + r + rt)
                g = jnp.where(r2_ref[hr(0), cl] < cnt[0], e2_ref[hr(0), cl] * w1[0], zero)
                for h in range(1, PEER_HEADS):
                    g = g + jnp.where(r2_ref[hr(h), cl] < cnt[h], e2_ref[hr(h), cl] * w1[h], zero)
                row = il * PEER_N_KEYS + r
                a = a_ref[row:row + rt, cl]
                act = 0.5 * a * (1.0 + lax.erf(a * (1.0 / math.sqrt(2.0))))
                ag_ref[row:row + rt, cl] = act.astype(BF16) * g

    @pl.when(e == pl.num_programs(1) - 1)
    def _():
        o_ref[...] = h_ref[...] + acc_ref[...].T


def _peer_call(h, xt, cnt, w1, r2, e2, lw):
    S = h.shape[0]
    tm = min(TM_PEER, S)
    n_exp = lw["wu"].shape[0]
    rows = ET_PEER // PEER_N_KEYS
    n_tiles = n_exp // ET_PEER
    cur = lambda e: jnp.minimum(e, n_tiles - 1)
    prev = lambda e: jnp.maximum(e - 1, 0)
    tile = pl.BlockSpec((PEER_HEADS, tm // LANES, rows, LANES), lambda j, e: (0, j, cur(e), 0))
    res = pl.BlockSpec((PEER_HEADS * PEER_N_KEYS, tm), lambda j, e: (0, j))
    return pl.pallas_call(
        _peer_kernel, grid=(S // tm, n_tiles + 1),
        in_specs=[pl.BlockSpec((tm, D_MODEL), lambda j, e: (j, 0)),
                  pl.BlockSpec((D_MODEL, tm), lambda j, e: (0, j)),
                  pl.BlockSpec((ET_PEER, D_MODEL), lambda j, e: (cur(e), 0)),
                  pl.BlockSpec((D_MODEL, ET_PEER), lambda j, e: (0, prev(e))),
                  tile, tile, res, res],
        out_specs=pl.BlockSpec((tm, D_MODEL), lambda j, e: (j, 0)),
        out_shape=jax.ShapeDtypeStruct((S, D_MODEL), F32),
        scratch_shapes=[pltpu.VMEM((D_MODEL, tm), F32), pltpu.VMEM((ET_PEER, tm), F32),
                        pltpu.VMEM((ET_PEER, tm), BF16),
                        pltpu.VMEM((PEER_HEADS * PEER_N_KEYS, tm), BF16),
                        pltpu.VMEM((PEER_HEADS * PEER_N_KEYS, tm), BF16)],
        compiler_params=_cparams(("arbitrary", "arbitrary")), name="peer",
    )(h, xt, lw["wu"], lw["wvt"], cnt, w1, r2.reshape(-1, S), e2.reshape(-1, S))


def _pad_heads(w, n_heads, width):
    lead = w.shape[:-1]
    w = w.reshape(*lead, n_heads, width)
    w = jnp.pad(w, [(0, 0)] * len(lead) + [(0, 0), (0, LANES - width)])
    return w.reshape(*lead, n_heads * LANES)


def _dup_kv(w):
    lead = w.shape[:-1]
    w = w.reshape(*lead, A_KV_HEADS, 1, HEAD_DIM)
    w = jnp.broadcast_to(w, (*lead, A_KV_HEADS, A_HEADS // A_KV_HEADS, HEAD_DIM))
    return w.reshape(*lead, A_HEADS * HEAD_DIM)


def _block_diag(width, segments):
    m = np.zeros((width, width), np.float32)
    for a, b, d in segments:
        m[a:b, a:b] = 1.0 / d
    return jnp.asarray(m, BF16)


def _consts():
    seg64 = [(i * 64, (i + 1) * 64, 64) for i in range(4)]
    seg128 = [(i * 128, (i + 1) * 128, 64) for i in range(2)]
    segc = []
    for i in range(2):
        segc += [(i * 128, i * 128 + 64, 64), (i * 128 + 64, (i + 1) * 128, 32)]
    return {
        "bd64": _block_diag(256, seg64), "bd128": _block_diag(256, seg128), "bdc": _block_diag(256, segc),
        "bd256": _block_diag(256, [(0, 256, 256)]), "bdkv": _block_diag(128, [(0, 128, 128)]),
    }


def _layer_weights(l, p):
    w = p["w_in"][l]
    cols = [w[:, IN_OFFS[i]:IN_OFFS[i + 1]] for i in range(len(IN_SPLITS))]
    aq, ak, av, bq, bk, bv, cq, ckv, ckr, dq, dk, dv, wgate = cols
    ckr_rep = jnp.zeros((D_MODEL, C_HEADS, LANES), F32).at[:, :, C_NOPE:C_QK_DIM].set(ckr[:, None, :])
    w1 = jnp.concatenate([
        aq, _dup_kv(ak), _dup_kv(av), bq, bk, bv, cq, ckv, ckr_rep.reshape(D_MODEL, C_HEADS * LANES),
        _pad_heads(dq, D_HEADS, HEAD_DIM), _pad_heads(dk, D_KV_HEADS, HEAD_DIM),
        _pad_heads(dv, D_KV_HEADS, HEAD_DIM)], axis=1).astype(BF16)

    scale = HEAD_DIM ** -0.5
    ones = lambda n: jnp.ones((n,), F32)
    ones_col = jnp.zeros((2, LANES), F32).at[:, HEAD_DIM].set(1.0).reshape(-1)
    ck_gain = p["c_k_gain"][l]
    ckr_gain = jnp.zeros((C_HEADS, LANES), F32).at[:, C_NOPE:C_QK_DIM].set(ck_gain[None, C_NOPE:])
    gains = jnp.concatenate([
        jnp.tile(p["a_q_gain"][l], 4) * scale, jnp.tile(p["a_k_gain"][l], 4), ones(256),
        jnp.tile(p["b_q_gain"][l], (1, 4)).reshape(-1) * scale, jnp.tile(p["b_k_gain"][l], (1, 4)).reshape(-1),
        ones(768),
        p["c_q_lat_gain"][l], p["c_kv_lat_gain"][l], ckr_gain.reshape(-1),
        jnp.tile(_pad_heads(p["d_q_gain"][l], 1, HEAD_DIM), 4) * (scale * LOG2E),
        jnp.tile(_pad_heads(p["d_k_gain"][l], 1, HEAD_DIM), 2), ones_col])[None, :]

    cq_gain = jnp.tile(_pad_heads(p["c_q_gain"][l], 1, C_QK_DIM), C_HEADS) * (C_QK_DIM ** -0.5 * LOG2E)
    ckn_gain = jnp.tile(_pad_heads(ck_gain[:C_NOPE], 1, C_NOPE), C_HEADS)
    wukv = p["c_w_ukv"][l].reshape(C_KV_RANK, C_HEADS, C_NOPE + C_V)
    wukv = jnp.concatenate([_pad_heads(wukv[:, :, :C_NOPE].reshape(C_KV_RANK, -1), C_HEADS, C_NOPE),
                            _pad_heads(wukv[:, :, C_NOPE:].reshape(C_KV_RANK, -1), C_HEADS, C_V)], axis=1)

    wb = p["w_branch"][l]
    pad_rows = lambda m: _pad_heads(m.T, 4, HEAD_DIM).T
    return {
        "g1": p["norm1_gain"][l][None, :], "w1": w1, "gains": gains,
        "wuq": _pad_heads(p["c_w_uq"][l], C_HEADS, C_QK_DIM).astype(BF16), "wukv": wukv.astype(BF16),
        "gcq": cq_gain[None, :], "gckn": ckn_gain[None, :],
        "wg": wgate.astype(BF16), "gb": p["gate_bias"][l][None, :],
        "wba": wb[0].astype(BF16), "wbb": wb[1].astype(BF16),
        "wbc": pad_rows(wb[2]).astype(BF16), "wbd": pad_rows(wb[3]).astype(BF16),
        "wo": p["w_out"][l].astype(BF16),
        "g2": p["norm2_gain"][l][None, :], "wqt": p["peer_w_q"][l].T.astype(BF16),
        "keys": p["peer_sub_keys"][l].astype(BF16),
        "wu": p["peer_w_u"][l].astype(BF16), "wvt": p["peer_w_v"][l].T.astype(BF16),
    }


def _rope_tables(S):
    half = 16
    freqs = ROPE_THETA ** (-jnp.arange(half, dtype=F32) / half)

    def cs(pos):
        ang = pos.astype(F32)[:, None] * freqs[None, :]
        return jnp.cos(ang), jnp.sin(ang)

    def table(parts):
        cos, sa, sb = [], [], []
        at = 0
        for start, (c, s) in parts:
            z = jnp.zeros((S, start - at), F32)
            zh = jnp.zeros((S, half), F32)
            cos += [z + 1.0, c, c]
            sa += [z, -s, zh]
            sb += [z, zh, s]
            at = start + 2 * half
        z = jnp.zeros((S, LANES - at), F32)
        cat = lambda pieces: jnp.concatenate(pieces, axis=1)
        return jnp.stack([cat(cos + [z + 1.0]), cat(sa + [z]), cat(sb + [z])])

    pos = jnp.arange(S, dtype=jnp.int32)
    tabc = table([(C_NOPE, cs(pos))])
    tabd = table([(0, cs(pos // GRID_W)), (2 * half, cs(pos % GRID_W))])
    return tabc, tabd


def _rel_bucket(rel):
    nb = REL_BUCKETS // 2
    max_exact = nb // 2
    ret = jnp.where(rel > 0, nb, 0)
    n = jnp.abs(rel)
    nf = jnp.maximum(n, 1).astype(F32)
    large = max_exact + (jnp.log(nf / max_exact) / math.log(REL_MAX_DIST / max_exact)
                         * (nb - max_exact)).astype(jnp.int32)
    large = jnp.minimum(large, nb - 1)
    return ret + jnp.where(n < max_exact, n, large)


def _band_bias(rel_table, half, dil):
    rel = jnp.arange(3 * QBLK)[None, :] - QBLK - jnp.arange(QBLK)[:, None]
    onehot = (_rel_bucket(rel * dil)[None, :, :] == jnp.arange(REL_BUCKETS)[:, None, None]).astype(F32)
    bias = jnp.einsum("bh,bqs->hqs", rel_table.astype(F32), onehot, precision=lax.Precision.HIGHEST)
    return jnp.where((jnp.abs(rel) <= half)[None], bias, NEG_INF)


def kernel(x, rel_bias, norm1_gain, w_in, gate_bias, a_q_gain, a_k_gain, a_sink, b_q_gain, b_k_gain, c_q_lat_gain, c_kv_lat_gain, c_w_uq, c_w_ukv, c_q_gain, c_k_gain, d_q_gain, d_k_gain, w_branch, w_out, norm2_gain, peer_w_q, peer_sub_keys, peer_w_u, peer_w_v):
    p = dict(rel_bias=rel_bias, norm1_gain=norm1_gain, w_in=w_in, gate_bias=gate_bias, a_q_gain=a_q_gain,
             a_k_gain=a_k_gain, a_sink=a_sink, b_q_gain=b_q_gain, b_k_gain=b_k_gain,
             c_q_lat_gain=c_q_lat_gain, c_kv_lat_gain=c_kv_lat_gain, c_w_uq=c_w_uq, c_w_ukv=c_w_ukv,
             c_q_gain=c_q_gain, c_k_gain=c_k_gain, d_q_gain=d_q_gain, d_k_gain=d_k_gain, w_branch=w_branch,
             w_out=w_out, norm2_gain=norm2_gain, peer_w_q=peer_w_q, peer_sub_keys=peer_sub_keys,
             peer_w_u=peer_w_u, peer_w_v=peer_w_v)
    Bn, S, D = x.shape
    assert Bn == 1 and D == D_MODEL and S % (16 * QBLK) == 0
    depth = w_in.shape[0]
    consts = _consts()
    tabc, tabd = _rope_tables(S)
    bias_a = _band_bias(rel_bias[:, :A_HEADS], A_HALF_WINDOW, 1)
    bias_b = [_band_bias(rel_bias[:, A_HEADS + g * B_HEADS:A_HEADS + (g + 1) * B_HEADS], win // (2 * dil), dil)
              for g, (win, dil) in enumerate(B_PATTERNS)]

    h = x.reshape(S, D)
    for l in range(depth):
        lw = _layer_weights(l, p)
        qa, ka, va, qb, kb, vb, qc, kct, vc, qd, kdt, vd = _proj_call(h, lw, consts, tabc, tabd)
        ya, _ = _banded_call(qa, ka, va, bias_a, a_sink[l], 1)
        obs, lses = [], []
        for g, (win, dil) in enumerate(B_PATTERNS):
            view = lambda t: t[:, g * 256:(g + 1) * 256].reshape(S // dil, dil * 256)
            o, lse = _banded_call(view(qb), view(kb), view(vb), bias_b[g], None, dil)
            obs.append(o.reshape(S, 256))
            lses.append(lse.reshape(S, 256))
        yc = _flash_call(qc, kct, vc, C_HEADS, 1)
        yd = _flash_call(qd, kdt, vd, D_HEADS, D_HEADS // D_KV_HEADS)
        h = _merge_call(h, lw, ya, obs, lses, yc, yd)
        xt, cnt, w1, r2, e2 = _peerprep_call(h, lw)
        h = _peer_call(h, xt, cnt, w1, r2, e2, lw)
    return h.reshape(Bn, S, D)
```

```python
import functools
import math

import jax
import jax.numpy as jnp
import numpy as np
from jax import lax
from jax.experimental import pallas as pl
from jax.experimental.pallas import tpu as pltpu

F32 = jnp.float32
BF16 = jnp.bfloat16

D_MODEL = 1024
HEAD_DIM = 64
GRID_W = 64
EPS = 1e-6
NEG_INF = -1e30
LOG2E = math.log2(math.e)
ROPE_THETA = 10000.0

A_HEADS, A_KV_HEADS, A_HALF_WINDOW = 4, 2, 128
B_PATTERNS = ((128, 1), (512, 4), (2048, 16))
B_GROUPS, B_HEADS = 3, 4
C_HEADS, C_Q_RANK, C_KV_RANK, C_NOPE, C_ROPE, C_V = 4, 256, 128, 64, 32, 64
C_QK_DIM = C_NOPE + C_ROPE
D_HEADS, D_KV_HEADS = 4, 2
N_BRANCHES = 4
REL_BUCKETS, REL_MAX_DIST = 32, 1024
PEER_HEADS, PEER_N_KEYS, PEER_KEY_DIM, PEER_TOPK = 8, 128, 128, 16
PEER_HALF = PEER_KEY_DIM // 2

IN_SPLITS = (256, 128, 128, 768, 768, 768, C_Q_RANK, C_KV_RANK, C_ROPE, 256, 128, 128, N_BRANCHES * D_MODEL)
IN_OFFS = tuple(int(o) for o in np.cumsum((0,) + IN_SPLITS))

LANES = 128
QBLK = 128
VMEM_LIMIT = 56 * 1024 * 1024

P_AQ, P_AK, P_AV = 0, 256, 512
P_BQ, P_BK, P_BV = 768, 1536, 2304
P_CQ, P_CKV, P_CKR = 3072, 3328, 3456
P_DQ, P_DK, P_DV = 3968, 4480, 4736
P_WIDTH = 4992

TM_PROJ = 256
TM_MERGE = 256
BAND_CHUNK = 2048
TQ_FLASH = 512
FLASH_CHUNKS = 4
TM_PREP = 256
TM_PEER = 512
PEER_ROW_TILE = 16
ET_PEER = 1024


def _cparams(sem):
    return pltpu.CompilerParams(dimension_semantics=sem, vmem_limit_bytes=VMEM_LIMIT)


def _full(shape):
    n = len(shape)
    return pl.BlockSpec(shape, lambda *_: (0,) * n)


def _seg_norm(t, bd, gain):
    ms = jnp.dot((t * t).astype(BF16), bd, preferred_element_type=F32)
    return t * lax.rsqrt(ms + EPS) * gain


def _rope128(x, tab_ref):
    up = pltpu.roll(x, LANES - 16, axis=1)
    dn = pltpu.roll(x, 16, axis=1)
    return x * tab_ref[0] + up * tab_ref[1] + dn * tab_ref[2]


def _proj_kernel(h_ref, g1_ref, w_ref, gains_ref, bd64_ref, bd128_ref, bdc_ref, bd256_ref, bdkv_ref,
                 wuq_ref, wukv_ref, gcq_ref, gckn_ref, tabc_ref, tabd_ref,
                 qa_ref, ka_ref, va_ref, qb_ref, kb_ref, vb_ref,
                 qc_ref, kct_ref, vc_ref, qd_ref, kdt_ref, vd_ref):
    x = h_ref[...]
    ms = jnp.mean(x * x, axis=-1, keepdims=True)
    hn = (x * lax.rsqrt(ms + EPS) * g1_ref[...]).astype(BF16)

    def proj(off, width):
        return jnp.dot(hn, w_ref[:, off:off + width], preferred_element_type=F32)

    def gain(off, width):
        return gains_ref[:, off:off + width]

    bd64 = bd64_ref[...]
    bd128 = bd128_ref[...]
    bdc = bdc_ref[...]
    ones_col = gain(P_DV, 256)

    qa_ref[...] = _seg_norm(proj(P_AQ, 256), bd64, gain(P_AQ, 256)).astype(BF16)
    ka_ref[...] = _seg_norm(proj(P_AK, 256), bd64, gain(P_AK, 256)).astype(BF16)
    va_ref[...] = proj(P_AV, 256).astype(BF16)

    for g in range(B_GROUPS):
        o = g * 256
        qb_ref[:, o:o + 256] = _seg_norm(proj(P_BQ + o, 256), bd64, gain(P_BQ + o, 256)).astype(BF16)
        kb_ref[:, o:o + 256] = _seg_norm(proj(P_BK + o, 256), bd64, gain(P_BK + o, 256)).astype(BF16)
        vb_ref[:, o:o + 256] = proj(P_BV + o, 256).astype(BF16)

    cq = _seg_norm(proj(P_CQ, 256), bd256_ref[...], gain(P_CQ, 256)).astype(BF16)
    ckv = _seg_norm(proj(P_CKV, 128), bdkv_ref[...], gain(P_CKV, 128)).astype(BF16)
    for hb in range(2):
        o = hb * 256
        q = jnp.dot(cq, wuq_ref[:, o:o + 256], preferred_element_type=F32)
        q = _seg_norm(q, bdc, gcq_ref[:, o:o + 256])
        kn = jnp.dot(ckv, wukv_ref[:, o:o + 256], preferred_element_type=F32)
        kn = _seg_norm(kn, bd128, gckn_ref[:, o:o + 256])
        kp = _seg_norm(proj(P_CKR + o, 256), bdc, gain(P_CKR + o, 256))
        vc_ref[:, o:o + 256] = (jnp.dot(ckv, wukv_ref[:, 512 + o:512 + o + 256], preferred_element_type=F32)
                                + ones_col).astype(BF16)
        for j in range(2):
            c = j * LANES
            hd = hb * 2 + j
            qc_ref[:, o + c:o + c + LANES] = _rope128(q[:, c:c + LANES], tabc_ref).astype(BF16)
            k = kn[:, c:c + LANES] + _rope128(kp[:, c:c + LANES], tabc_ref)
            kct_ref[hd, 0] = k.T.astype(BF16)

    for hb in range(2):
        o = hb * 256
        q = _seg_norm(proj(P_DQ + o, 256), bd128, gain(P_DQ + o, 256))
        for j in range(2):
            c = j * LANES
            qd_ref[:, o + c:o + c + LANES] = _rope128(q[:, c:c + LANES], tabd_ref).astype(BF16)
    k = _seg_norm(proj(P_DK, 256), bd128, gain(P_DK, 256))
    for j in range(D_KV_HEADS):
        c = j * LANES
        kdt_ref[j, 0] = _rope128(k[:, c:c + LANES], tabd_ref).T.astype(BF16)
    vd_ref[...] = (proj(P_DV, 256) + ones_col).astype(BF16)


def _proj_call(h, lw, consts, tabc, tabd):
    S = h.shape[0]
    tm = TM_PROJ
    nt = S // tm
    row = lambda w: pl.BlockSpec((tm, w), lambda i: (i, 0))
    in_specs = [
        row(D_MODEL), _full((1, D_MODEL)), _full((D_MODEL, P_WIDTH)), _full((1, P_WIDTH)),
        _full((256, 256)), _full((256, 256)), _full((256, 256)), _full((256, 256)), _full((128, 128)),
        _full((C_Q_RANK, 512)), _full((C_KV_RANK, 1024)), _full((1, 512)), _full((1, 512)),
        pl.BlockSpec((3, tm, LANES), lambda i: (0, i, 0)),
        pl.BlockSpec((3, tm, LANES), lambda i: (0, i, 0)),
    ]
    out_shape = [
        jax.ShapeDtypeStruct((S, 256), BF16), jax.ShapeDtypeStruct((S, 256), BF16),
        jax.ShapeDtypeStruct((S, 256), BF16),
        jax.ShapeDtypeStruct((S, 768), BF16), jax.ShapeDtypeStruct((S, 768), BF16),
        jax.ShapeDtypeStruct((S, 768), BF16),
        jax.ShapeDtypeStruct((S, 512), BF16), jax.ShapeDtypeStruct((C_HEADS, nt, LANES, tm), BF16),
        jax.ShapeDtypeStruct((S, 512), BF16),
        jax.ShapeDtypeStruct((S, 512), BF16), jax.ShapeDtypeStruct((D_KV_HEADS, nt, LANES, tm), BF16),
        jax.ShapeDtypeStruct((S, 256), BF16),
    ]
    out_specs = [
        row(256), row(256), row(256), row(768), row(768), row(768),
        row(512), pl.BlockSpec((C_HEADS, 1, LANES, tm), lambda i: (0, i, 0, 0)), row(512),
        row(512), pl.BlockSpec((D_KV_HEADS, 1, LANES, tm), lambda i: (0, i, 0, 0)), row(256),
    ]
    return pl.pallas_call(
        _proj_kernel, grid=(nt,), in_specs=in_specs, out_specs=out_specs, out_shape=out_shape,
        compiler_params=_cparams(("parallel",)), name="proj",
    )(h, lw["g1"], lw["w1"], lw["gains"], consts["bd64"], consts["bd128"], consts["bdc"],
      consts["bd256"], consts["bdkv"], lw["wuq"], lw["wukv"], lw["gcq"], lw["gckn"], tabc, tabd)


def _banded_kernel(*refs, seq_len, has_sink):
    if has_sink:
        sink_ref, refs = refs[0], refs[1:]
    (q_ref, kp_ref, km_ref, kn_ref, vp_ref, vm_ref, vn_ref, bias_ref, o_ref, lse_ref,
     kf, vf, s_buf, p_buf, d_buf, l_buf) = refs
    chunk = q_ref.shape[0]
    nb = chunk // QBLK
    first_blk = pl.program_id(1) * nb
    for dst, parts in ((kf, (kp_ref, km_ref, kn_ref)), (vf, (vp_ref, vm_ref, vn_ref))):
        dst[0:QBLK, :] = parts[0][...]
        dst[QBLK:QBLK + chunk, :] = parts[1][...]
        dst[QBLK + chunk:, :] = parts[2][...]
    lane = lax.broadcasted_iota(jnp.int32, (1, 4 * HEAD_DIM), 1)
    hmasks = [(lane >= h * HEAD_DIM) & (lane < (h + 1) * HEAD_DIM) for h in range(4)]
    key_iota = lax.broadcasted_iota(jnp.int32, (1, 3 * QBLK), 1)

    def blk_rows(j):
        return pl.ds(pl.multiple_of(j * QBLK, QBLK), QBLK)

    def halo_rows(j):
        return pl.ds(pl.multiple_of(j * QBLK, QBLK), 3 * QBLK)

    def scores(j, slot):
        q = q_ref[blk_rows(j), :]
        q4 = jnp.concatenate([jnp.where(hm, q, jnp.zeros_like(q)) for hm in hmasks], axis=0)
        s_buf[slot] = lax.dot_general(q4, kf[halo_rows(j), :], (((1,), (1,)), ((), ())),
                                      preferred_element_type=F32)

    def softmax(j, slot):
        kpos = (first_blk + j - 1) * QBLK + key_iota
        valid = (kpos >= 0) & (kpos < seq_len)
        s = jnp.where(valid, s_buf[slot] + bias_ref[...], NEG_INF)
        m = jnp.max(s, axis=-1, keepdims=True)
        if has_sink:
            m = jnp.maximum(m, sink_ref[...])
        p = jnp.exp(s - m)
        denom = jnp.sum(p, axis=-1, keepdims=True)
        if has_sink:
            denom = denom + jnp.exp(sink_ref[...] - m)
        p_buf[slot] = p.astype(BF16)
        d_buf[slot] = 1.0 / denom
        l_buf[slot] = m + jnp.log(denom)

    def output(j, slot):
        pv = jnp.dot(p_buf[slot], vf[halo_rows(j), :], preferred_element_type=F32) * d_buf[slot]
        lse4 = l_buf[slot]
        out = jnp.zeros((QBLK, 4 * HEAD_DIM), F32)
        lse = jnp.zeros((QBLK, 4 * HEAD_DIM), F32)
        for h in range(4):
            hr = slice(h * QBLK, (h + 1) * QBLK)
            out = out + jnp.where(hmasks[h], pv[hr], 0.0)
            lse = lse + jnp.where(hmasks[h], lse4[hr], 0.0)
        o_ref[blk_rows(j), :] = out.astype(o_ref.dtype)
        lse_ref[blk_rows(j), :] = lse

    scores(0, 0)
    scores(1, 1)
    softmax(0, 0)

    def body(i, carry):
        j = 2 * i
        scores(j, 0)
        softmax(j - 1, 1)
        output(j - 2, 0)
        scores(j + 1, 1)
        softmax(j, 0)
        output(j - 1, 1)
        return carry

    lax.fori_loop(1, nb // 2, body, 0)
    softmax(nb - 1, 1)
    output(nb - 2, 0)
    output(nb - 1, 1)


def _banded_call(q, k, v, bias, sink, dil):
    L = q.shape[0]
    chunk = min(BAND_CHUNK, L)
    nb = chunk // QBLK
    n_blocks = L // QBLK
    assert L % chunk == 0 and nb % 2 == 0 and nb >= 4
    main = pl.BlockSpec((chunk, 256), lambda r, c: (c, r))
    halo = lambda off: pl.BlockSpec((QBLK, 256), lambda r, c: (jnp.clip(c * nb + off, 0, n_blocks - 1), r))
    kv = [halo(-1), main, halo(nb)]
    in_specs = [main] + kv + kv + [_full((4 * QBLK, 3 * QBLK))]
    args = [q, k, k, k, v, v, v, bias.reshape(4 * QBLK, 3 * QBLK)]
    if sink is not None:
        in_specs = [_full((4 * QBLK, 1))] + in_specs
        args = [jnp.repeat(sink.astype(F32), QBLK)[:, None]] + args
    stacked = 4 * QBLK
    return pl.pallas_call(
        functools.partial(_banded_kernel, seq_len=L, has_sink=sink is not None),
        grid=(dil, L // chunk), in_specs=in_specs,
        out_specs=[main, main],
        out_shape=[jax.ShapeDtypeStruct(q.shape, BF16), jax.ShapeDtypeStruct(q.shape, F32)],
        scratch_shapes=[pltpu.VMEM((chunk + 2 * QBLK, 256), BF16), pltpu.VMEM((chunk + 2 * QBLK, 256), BF16),
                        pltpu.VMEM((2, stacked, 3 * QBLK), F32), pltpu.VMEM((2, stacked, 3 * QBLK), BF16),
                        pltpu.VMEM((2, stacked, 1), F32), pltpu.VMEM((2, stacked, 1), F32)],
        compiler_params=_cparams(("parallel", "parallel")), name="banded",
    )(*args)


def _flash_kernel(q_ref, kt_ref, v_ref, o_ref, s_buf, p_buf, a_buf, m_ref, *, n_chunks, chunk, tq):
    per = FLASH_CHUNKS
    n_k = n_chunks // per
    width = per * chunk
    total = (q_ref.shape[0] // tq) * n_k

    def rows(t):
        return pl.ds(pl.multiple_of((t // n_k) * tq, tq), tq)

    def scores(t, slot):
        ks = t % n_k
        kt = jnp.concatenate([kt_ref[ks * per + j] for j in range(per)], axis=1)
        s_buf[slot] = jnp.dot(q_ref[rows(t), :], kt, preferred_element_type=F32)

    def softmax(t, slot):
        s = s_buf[slot]
        m = jnp.where(t % n_k == 0, -jnp.inf, m_ref[...])
        m_new = jnp.maximum(m, jnp.max(s, axis=-1, keepdims=True))
        m_ref[...] = m_new
        a_buf[slot] = jnp.exp2(m - m_new)
        p_buf[slot] = jnp.exp2(s - m_new).astype(BF16)

    def accumulate(t, slot):
        ks = t % n_k
        vv = v_ref[pl.ds(pl.multiple_of(ks * width, width), width), :]
        r = rows(t)
        o_ref[r, :] = a_buf[slot] * o_ref[r, :] + jnp.dot(p_buf[slot], vv, preferred_element_type=F32)

    o_ref[...] = jnp.zeros(o_ref.shape, F32)
    m_ref[...] = jnp.full(m_ref.shape, -jnp.inf, F32)
    scores(0, 0)
    scores(1, 1)
    softmax(0, 0)

    def body(i, carry):
        t = 2 * i
        scores(t, 0)
        softmax(t - 1, 1)
        accumulate(t - 2, 0)
        scores(t + 1, 1)
        softmax(t, 0)
        accumulate(t - 1, 1)
        return carry

    lax.fori_loop(1, total // 2, body, 0)
    softmax(total - 1, 1)
    accumulate(total - 2, 0)
    accumulate(total - 1, 1)


def _flash_call(q, kt, v, n_heads, group):
    S = q.shape[0]
    n_chunks, chunk = kt.shape[1], kt.shape[3]
    tq = min(TQ_FLASH, S)
    width = FLASH_CHUNKS * chunk
    assert n_chunks % (2 * FLASH_CHUNKS) == 0 and S % tq == 0
    once = dict(pipeline_mode=pl.Buffered(1))
    return pl.pallas_call(
        functools.partial(_flash_kernel, n_chunks=n_chunks, chunk=chunk, tq=tq),
        grid=(n_heads,),
        in_specs=[pl.BlockSpec((S, LANES), lambda h: (0, h), **once),
                  pl.BlockSpec((None, n_chunks, LANES, chunk), lambda h: (h // group, 0, 0, 0), **once),
                  pl.BlockSpec((S, LANES), lambda h: (0, h // group), **once)],
        out_specs=pl.BlockSpec((S, LANES), lambda h: (0, h)),
        out_shape=jax.ShapeDtypeStruct((S, n_heads * LANES), F32),
        scratch_shapes=[pltpu.VMEM((2, tq, width), F32), pltpu.VMEM((2, tq, width), BF16),
                        pltpu.VMEM((2, tq, 1), F32), pltpu.VMEM((tq, 1), F32)],
        compiler_params=_cparams(("parallel",)), name="flash",
    )(q, kt, v)


def _merge_kernel(h_ref, g1_ref, wg_ref, gb_ref, ya_ref, ob1_ref, ob2_ref, ob3_ref,
                  ls1_ref, ls2_ref, ls3_ref, yc_ref, yd_ref,
                  wba_ref, wbb_ref, wbc_ref, wbd_ref, wo_ref, o_ref):
    x = h_ref[...]
    ms = jnp.mean(x * x, axis=-1, keepdims=True)
    hn = (x * lax.rsqrt(ms + EPS) * g1_ref[...]).astype(BF16)

    l1, l2, l3 = ls1_ref[...], ls2_ref[...], ls3_ref[...]
    mx = jnp.maximum(jnp.maximum(l1, l2), l3)
    e1, e2, e3 = jnp.exp(l1 - mx), jnp.exp(l2 - mx), jnp.exp(l3 - mx)
    yb = (e1 * ob1_ref[...].astype(F32) + e2 * ob2_ref[...].astype(F32)
          + e3 * ob3_ref[...].astype(F32)) / (e1 + e2 + e3)

    lane = lax.broadcasted_iota(jnp.int32, (1, LANES), 1)

    def normalised(acc_ref):
        heads = []
        for hb in range(acc_ref.shape[1] // LANES):
            blk = acc_ref[:, hb * LANES:(hb + 1) * LANES]
            heads.append(jnp.where(lane < HEAD_DIM, blk / blk[:, HEAD_DIM:HEAD_DIM + 1], 0.0).astype(BF16))
        return jnp.concatenate(heads, axis=1)

    ys = (ya_ref[...], yb.astype(BF16), normalised(yc_ref), normalised(yd_ref))
    wbs = (wba_ref, wbb_ref, wbc_ref, wbd_ref)
    merged = jnp.zeros((x.shape[0], D_MODEL), F32)
    for n in range(N_BRANCHES):
        o = n * D_MODEL
        gpre = jnp.dot(hn, wg_ref[:, o:o + D_MODEL], preferred_element_type=F32) + gb_ref[:, o:o + D_MODEL]
        br = jnp.dot(ys[n], wbs[n][...], preferred_element_type=F32)
        merged = merged + jax.nn.sigmoid(gpre) * br
    o_ref[...] = x + jnp.dot(merged.astype(BF16), wo_ref[...], preferred_element_type=F32)


def _merge_call(h, lw, ya, obs, lses, yc, yd):
    S = h.shape[0]
    tm = TM_MERGE
    row = lambda w: pl.BlockSpec((tm, w), lambda i: (i, 0))
    in_specs = [row(D_MODEL), _full((1, D_MODEL)), _full((D_MODEL, 4 * D_MODEL)), _full((1, 4 * D_MODEL)),
                row(256), row(256), row(256), row(256), row(256), row(256), row(256), row(512), row(512),
                _full((256, D_MODEL)), _full((256, D_MODEL)), _full((512, D_MODEL)), _full((512, D_MODEL)),
                _full((D_MODEL, D_MODEL))]
    return pl.pallas_call(
        _merge_kernel, grid=(S // tm,), in_specs=in_specs, out_specs=row(D_MODEL),
        out_shape=jax.ShapeDtypeStruct((S, D_MODEL), F32),
        compiler_params=_cparams(("parallel",)), name="merge",
    )(h, lw["g1"], lw["wg"], lw["gb"], ya, obs[0], obs[1], obs[2], lses[0], lses[1], lses[2], yc, yd,
      lw["wba"], lw["wbb"], lw["wbc"], lw["wbd"], lw["wo"])


def _top16_desc(s, want_rank=False):
    vals = []
    cur = s
    rank = jnp.full(s.shape, float(PEER_TOPK), F32)
    for k in range(PEER_TOPK):
        m = jnp.max(cur, axis=0, keepdims=True)
        vals.append(m)
        hit = cur == m
        if want_rank:
            rank = jnp.where(hit, float(k), rank)
        cur = jnp.where(hit, -jnp.inf, cur)
    return (vals, rank) if want_rank else vals


def _stack_rows(rows):
    n = len(rows)
    ridx = lax.broadcasted_iota(jnp.int32, (n, rows[0].shape[1]), 0)
    out = jnp.zeros((n, rows[0].shape[1]), F32)
    for r, v in enumerate(rows):
        out = jnp.where(ridx == r, v, out)
    return out


def _dup_bf16(x):
    u = pltpu.bitcast(x.astype(BF16).astype(F32), jnp.uint32)
    return u | (u >> 16)


def _peerprep_kernel(h_ref, g2_ref, wqt_ref, keys_ref, xt_ref, cnt_ref, w1_ref, r2_ref, e2_ref):
    x = h_ref[...]
    ms = jnp.mean(x * x, axis=-1, keepdims=True)
    xn = x * lax.rsqrt(ms + EPS) * g2_ref[...]
    xt = xn.T.astype(BF16)
    xt_ref[...] = xt
    qt = jnp.dot(wqt_ref[...], xt, preferred_element_type=F32).astype(BF16)
    for h in range(PEER_HEADS):
        o = h * PEER_KEY_DIM
        s1 = jnp.dot(keys_ref[h, 0], qt[o:o + PEER_HALF], preferred_element_type=F32)
        s2 = jnp.dot(keys_ref[h, 1], qt[o + PEER_HALF:o + PEER_KEY_DIM], preferred_element_type=F32)
        v1 = _top16_desc(s1)
        v2l, rank2 = _top16_desc(s2, want_rank=True)
        v2 = _stack_rows(v2l)
        cand = jnp.concatenate([v1[0] + v2] + [v1[a] + v2[0:8] for a in range(1, PEER_TOPK)], axis=0)
        top = _top16_desc(cand)
        t0, tau = top[0], top[PEER_TOPK - 1]
        z = jnp.exp(top[0] - t0)
        for t in top[1:]:
            z = z + jnp.exp(t - t0)
        cnt = jnp.zeros(s1.shape, F32)
        for b in range(PEER_TOPK):
            cnt = cnt + jnp.where(s1 + v2l[b] >= tau, 1.0, 0.0)
        cnt = _dup_bf16(cnt)
        w1 = _dup_bf16(jnp.exp(s1 - v1[0]) / z)
        for c in range(x.shape[0] // LANES):
            cnt_ref[h, c] = cnt[:, c * LANES:(c + 1) * LANES]
            w1_ref[h, c] = w1[:, c * LANES:(c + 1) * LANES]
        r2_ref[h] = rank2.astype(BF16)
        e2_ref[h] = jnp.exp(s2 - v2l[0]).astype(BF16)


def _peerprep_call(h, lw):
    S = h.shape[0]
    tm = min(TM_PREP, S)
    hs = pl.BlockSpec((PEER_HEADS, PEER_N_KEYS, tm), lambda i: (0, 0, i))
    hshape = jax.ShapeDtypeStruct((PEER_HEADS, PEER_N_KEYS, S), BF16)
    cs = pl.BlockSpec((PEER_HEADS, tm // LANES, PEER_N_KEYS, LANES), lambda i: (0, i, 0, 0))
    cshape = jax.ShapeDtypeStruct((PEER_HEADS, S // LANES, PEER_N_KEYS, LANES), jnp.uint32)
    return pl.pallas_call(
        _peerprep_kernel, grid=(S // tm,),
        in_specs=[pl.BlockSpec((tm, D_MODEL), lambda i: (i, 0)), _full((1, D_MODEL)),
                  _full((D_MODEL, D_MODEL)), _full((PEER_HEADS, 2, PEER_N_KEYS, PEER_HALF))],
        out_specs=[pl.BlockSpec((D_MODEL, tm), lambda i: (0, i)), cs, cs, hs, hs],
        out_shape=[jax.ShapeDtypeStruct((D_MODEL, S), BF16), cshape, cshape, hshape, hshape],
        compiler_params=_cparams(("parallel",)), name="peerprep",
    )(h, lw["g2"], lw["wqt"], lw["keys"])


def _replicated_row(ref, h, c, row):
    return ref[h, c, pl.ds(row, 8, stride=0), :]


def _peer_kernel(h_ref, xt_ref, wu_ref, wvt_ref, cnt_ref, w1_ref, r2_in, e2_in, o_ref,
                 acc_ref, a_ref, ag_ref, r2_ref, e2_ref):
    j, e = pl.program_id(0), pl.program_id(1)

    @pl.when((j == 0) & (e == 0))
    def _():
        ag_ref[...] = jnp.zeros_like(ag_ref)
        acc_ref[...] = jnp.zeros_like(acc_ref)

    @pl.when(e == 0)
    def _():
        r2_ref[...] = r2_in[...]
        e2_ref[...] = e2_in[...]

    contrib = jnp.dot(wvt_ref[...], ag_ref[...], preferred_element_type=F32)
    acc_ref[...] = jnp.where(e == 0, 0.0, acc_ref[...] + contrib)

    a_ref[...] = jnp.dot(wu_ref[...], xt_ref[...], preferred_element_type=F32)
    rt = PEER_ROW_TILE
    zero = jnp.zeros((rt, LANES), BF16)
    for il in range(ET_PEER // PEER_N_KEYS):
        for c in range(a_ref.shape[1] // LANES):
            cl = slice(c * LANES, (c + 1) * LANES)
            cnt = [pltpu.bitcast(_replicated_row(cnt_ref, h, c, il), BF16) for h in range(PEER_HEADS)]
            w1 = [pltpu.bitcast(_replicated_row(w1_ref, h, c, il), BF16) for h in range(PEER_HEADS)]
            for r in range(0, PEER_N_KEYS, rt):
                hr = lambda h: slice(h * PEER_N_KEYS + r, h * PEER_N_KEYS + r + rt)
                g = jnp.where(r2_ref[hr(0), cl] < cnt[0], e2_ref[hr(0), cl] * w1[0], zero)
                for h in range(1, PEER_HEADS):
                    g = g + jnp.where(r2_ref[hr(h), cl] < cnt[h], e2_ref[hr(h), cl] * w1[h], zero)
                row = il * PEER_N_KEYS + r
                a = a_ref[row:row + rt, cl]
                act = 0.5 * a * (1.0 + lax.erf(a * (1.0 / math.sqrt(2.0))))
                ag_ref[row:row + rt, cl] = act.astype(BF16) * g

    @pl.when(e == pl.num_programs(1) - 1)
    def _():
        o_ref[...] = h_ref[...] + acc_ref[...].T


def _peer_call(h, xt, cnt, w1, r2, e2, lw):
    S = h.shape[0]
    tm = min(TM_PEER, S)
    n_exp = lw["wu"].shape[0]
    rows = ET_PEER // PEER_N_KEYS
    n_tiles = n_exp // ET_PEER
    cur = lambda e: jnp.minimum(e, n_tiles - 1)
    prev = lambda e: jnp.maximum(e - 1, 0)
    tile = pl.BlockSpec((PEER_HEADS, tm // LANES, rows, LANES), lambda j, e: (0, j, cur(e), 0))
    res = pl.BlockSpec((PEER_HEADS * PEER_N_KEYS, tm), lambda j, e: (0, j))
    return pl.pallas_call(
        _peer_kernel, grid=(S // tm, n_tiles + 1),
        in_specs=[pl.BlockSpec((tm, D_MODEL), lambda j, e: (j, 0)),
                  pl.BlockSpec((D_MODEL, tm), lambda j, e: (0, j)),
                  pl.BlockSpec((ET_PEER, D_MODEL), lambda j, e: (cur(e), 0)),
                  pl.BlockSpec((D_MODEL, ET_PEER), lambda j, e: (0, prev(e))),
                  tile, tile, res, res],
        out_specs=pl.BlockSpec((tm, D_MODEL), lambda j, e: (j, 0)),
        out_shape=jax.ShapeDtypeStruct((S, D_MODEL), F32),
        scratch_shapes=[pltpu.VMEM((D_MODEL, tm), F32), pltpu.VMEM((ET_PEER, tm), F32),
                        pltpu.VMEM((ET_PEER, tm), BF16),
                        pltpu.VMEM((PEER_HEADS * PEER_N_KEYS, tm), BF16),
                        pltpu.VMEM((PEER_HEADS * PEER_N_KEYS, tm), BF16)],
        compiler_params=_cparams(("arbitrary", "arbitrary")), name="peer",
    )(h, xt, lw["wu"], lw["wvt"], cnt, w1, r2.reshape(-1, S), e2.reshape(-1, S))


def _pad_heads(w, n_heads, width):
    lead = w.shape[:-1]
    w = w.reshape(*lead, n_heads, width)
    w = jnp.pad(w, [(0, 0)] * len(lead) + [(0, 0), (0, LANES - width)])
    return w.reshape(*lead, n_heads * LANES)


def _dup_kv(w):
    lead = w.shape[:-1]
    w = w.reshape(*lead, A_KV_HEADS, 1, HEAD_DIM)
    w = jnp.broadcast_to(w, (*lead, A_KV_HEADS, A_HEADS // A_KV_HEADS, HEAD_DIM))
    return w.reshape(*lead, A_HEADS * HEAD_DIM)


def _block_diag(width, segments):
    m = np.zeros((width, width), np.float32)
    for a, b, d in segments:
        m[a:b, a:b] = 1.0 / d
    return jnp.asarray(m, BF16)


def _consts():
    seg64 = [(i * 64, (i + 1) * 64, 64) for i in range(4)]
    seg128 = [(i * 128, (i + 1) * 128, 64) for i in range(2)]
    segc = []
    for i in range(2):
        segc += [(i * 128, i * 128 + 64, 64), (i * 128 + 64, (i + 1) * 128, 32)]
    return {
        "bd64": _block_diag(256, seg64), "bd128": _block_diag(256, seg128), "bdc": _block_diag(256, segc),
        "bd256": _block_diag(256, [(0, 256, 256)]), "bdkv": _block_diag(128, [(0, 128, 128)]),
    }


def _layer_weights(l, p):
    w = p["w_in"][l]
    cols = [w[:, IN_OFFS[i]:IN_OFFS[i + 1]] for i in range(len(IN_SPLITS))]
    aq, ak, av, bq, bk, bv, cq, ckv, ckr, dq, dk, dv, wgate = cols
    ckr_rep = jnp.zeros((D_MODEL, C_HEADS, LANES), F32).at[:, :, C_NOPE:C_QK_DIM].set(ckr[:, None, :])
    w1 = jnp.concatenate([
        aq, _dup_kv(ak), _dup_kv(av), bq, bk, bv, cq, ckv, ckr_rep.reshape(D_MODEL, C_HEADS * LANES),
        _pad_heads(dq, D_HEADS, HEAD_DIM), _pad_heads(dk, D_KV_HEADS, HEAD_DIM),
        _pad_heads(dv, D_KV_HEADS, HEAD_DIM)], axis=1).astype(BF16)

    scale = HEAD_DIM ** -0.5
    ones = lambda n: jnp.ones((n,), F32)
    ones_col = jnp.zeros((2, LANES), F32).at[:, HEAD_DIM].set(1.0).reshape(-1)
    ck_gain = p["c_k_gain"][l]
    ckr_gain = jnp.zeros((C_HEADS, LANES), F32).at[:, C_NOPE:C_QK_DIM].set(ck_gain[None, C_NOPE:])
    gains = jnp.concatenate([
        jnp.tile(p["a_q_gain"][l], 4) * scale, jnp.tile(p["a_k_gain"][l], 4), ones(256),
        jnp.tile(p["b_q_gain"][l], (1, 4)).reshape(-1) * scale, jnp.tile(p["b_k_gain"][l], (1, 4)).reshape(-1),
        ones(768),
        p["c_q_lat_gain"][l], p["c_kv_lat_gain"][l], ckr_gain.reshape(-1),
        jnp.tile(_pad_heads(p["d_q_gain"][l], 1, HEAD_DIM), 4) * (scale * LOG2E),
        jnp.tile(_pad_heads(p["d_k_gain"][l], 1, HEAD_DIM), 2), ones_col])[None, :]

    cq_gain = jnp.tile(_pad_heads(p["c_q_gain"][l], 1, C_QK_DIM), C_HEADS) * (C_QK_DIM ** -0.5 * LOG2E)
    ckn_gain = jnp.tile(_pad_heads(ck_gain[:C_NOPE], 1, C_NOPE), C_HEADS)
    wukv = p["c_w_ukv"][l].reshape(C_KV_RANK, C_HEADS, C_NOPE + C_V)
    wukv = jnp.concatenate([_pad_heads(wukv[:, :, :C_NOPE].reshape(C_KV_RANK, -1), C_HEADS, C_NOPE),
                            _pad_heads(wukv[:, :, C_NOPE:].reshape(C_KV_RANK, -1), C_HEADS, C_V)], axis=1)

    wb = p["w_branch"][l]
    pad_rows = lambda m: _pad_heads(m.T, 4, HEAD_DIM).T
    return {
        "g1": p["norm1_gain"][l][None, :], "w1": w1, "gains": gains,
        "wuq": _pad_heads(p["c_w_uq"][l], C_HEADS, C_QK_DIM).astype(BF16), "wukv": wukv.astype(BF16),
        "gcq": cq_gain[None, :], "gckn": ckn_gain[None, :],
        "wg": wgate.astype(BF16), "gb": p["gate_bias"][l][None, :],
        "wba": wb[0].astype(BF16), "wbb": wb[1].astype(BF16),
        "wbc": pad_rows(wb[2]).astype(BF16), "wbd": pad_rows(wb[3]).astype(BF16),
        "wo": p["w_out"][l].astype(BF16),
        "g2": p["norm2_gain"][l][None, :], "wqt": p["peer_w_q"][l].T.astype(BF16),
        "keys": p["peer_sub_keys"][l].astype(BF16),
        "wu": p["peer_w_u"][l].astype(BF16), "wvt": p["peer_w_v"][l].T.astype(BF16),
    }


def _rope_tables(S):
    half = 16
    freqs = ROPE_THETA ** (-jnp.arange(half, dtype=F32) / half)

    def cs(pos):
        ang = pos.astype(F32)[:, None] * freqs[None, :]
        return jnp.cos(ang), jnp.sin(ang)

    def table(parts):
        cos, sa, sb = [], [], []
        at = 0
        for start, (c, s) in parts:
            z = jnp.zeros((S, start - at), F32)
            zh = jnp.zeros((S, half), F32)
            cos += [z + 1.0, c, c]
            sa += [z, -s, zh]
            sb += [z, zh, s]
            at = start + 2 * half
        z = jnp.zeros((S, LANES - at), F32)
        cat = lambda pieces: jnp.concatenate(pieces, axis=1)
        return jnp.stack([cat(cos + [z + 1.0]), cat(sa + [z]), cat(sb + [z])])

    pos = jnp.arange(S, dtype=jnp.int32)
    tabc = table([(C_NOPE, cs(pos))])
    tabd = table([(0, cs(pos // GRID_W)), (2 * half, cs(pos % GRID_W))])
    return tabc, tabd


def _rel_bucket(rel):
    nb = REL_BUCKETS // 2
    max_exact = nb // 2
    ret = jnp.where(rel > 0, nb, 0)
    n = jnp.abs(rel)
    nf = jnp.maximum(n, 1).astype(F32)
    large = max_exact + (jnp.log(nf / max_exact) / math.log(REL_MAX_DIST / max_exact)
                         * (nb - max_exact)).astype(jnp.int32)
    large = jnp.minimum(large, nb - 1)
    return ret + jnp.where(n < max_exact, n, large)


def _band_bias(rel_table, half, dil):
    rel = jnp.arange(3 * QBLK)[None, :] - QBLK - jnp.arange(QBLK)[:, None]
    onehot = (_rel_bucket(rel * dil)[None, :, :] == jnp.arange(REL_BUCKETS)[:, None, None]).astype(F32)
    bias = jnp.einsum("bh,bqs->hqs", rel_table.astype(F32), onehot, precision=lax.Precision.HIGHEST)
    return jnp.where((jnp.abs(rel) <= half)[None], bias, NEG_INF)


def kernel(x, rel_bias, norm1_gain, w_in, gate_bias, a_q_gain, a_k_gain, a_sink, b_q_gain, b_k_gain, c_q_lat_gain, c_kv_lat_gain, c_w_uq, c_w_ukv, c_q_gain, c_k_gain, d_q_gain, d_k_gain, w_branch, w_out, norm2_gain, peer_w_q, peer_sub_keys, peer_w_u, peer_w_v):
    p = dict(rel_bias=rel_bias, norm1_gain=norm1_gain, w_in=w_in, gate_bias=gate_bias, a_q_gain=a_q_gain,
             a_k_gain=a_k_gain, a_sink=a_sink, b_q_gain=b_q_gain, b_k_gain=b_k_gain,
             c_q_lat_gain=c_q_lat_gain, c_kv_lat_gain=c_kv_lat_gain, c_w_uq=c_w_uq, c_w_ukv=c_w_ukv,
             c_q_gain=c_q_gain, c_k_gain=c_k_gain, d_q_gain=d_q_gain, d_k_gain=d_k_gain, w_branch=w_branch,
             w_out=w_out, norm2_gain=norm2_gain, peer_w_q=peer_w_q, peer_sub_keys=peer_sub_keys,
             peer_w_u=peer_w_u, peer_w_v=peer_w_v)
    Bn, S, D = x.shape
    assert Bn == 1 and D == D_MODEL and S % (16 * QBLK) == 0
    depth = w_in.shape[0]
    consts = _consts()
    tabc, tabd = _rope_tables(S)
    bias_a = _band_bias(rel_bias[:, :A_HEADS], A_HALF_WINDOW, 1)
    bias_b = [_band_bias(rel_bias[:, A_HEADS + g * B_HEADS:A_HEADS + (g + 1) * B_HEADS], win // (2 * dil), dil)
              for g, (win, dil) in enumerate(B_PATTERNS)]

    h = x.reshape(S, D)
    for l in range(depth):
        lw = _layer_weights(l, p)
        qa, ka, va, qb, kb, vb, qc, kct, vc, qd, kdt, vd = _proj_call(h, lw, consts, tabc, tabd)
        ya, _ = _banded_call(qa, ka, va, bias_a, a_sink[l], 1)
        obs, lses = [], []
        for g, (win, dil) in enumerate(B_PATTERNS):
            view = lambda t: t[:, g * 256:(g + 1) * 256].reshape(S // dil, dil * 256)
            o, lse = _banded_call(view(qb), view(kb), view(vb), bias_b[g], None, dil)
            obs.append(o.reshape(S, 256))
            lses.append(lse.reshape(S, 256))
        yc = _flash_call(qc, kct, vc, C_HEADS, 1)
        yd = _flash_call(qd, kdt, vd, D_HEADS, D_HEADS // D_KV_HEADS)
        h = _merge_call(h, lw, ya, obs, lses, yc, yd)
        xt, cnt, w1, r2, e2 = _peerprep_call(h, lw)
        h = _peer_call(h, xt, cnt, w1, r2, e2, lw)
    return h.reshape(Bn, S, D)
```

```python
import functools
import math

import jax
import jax.numpy as jnp
import numpy as np
from jax import lax
from jax.experimental import pallas as pl
from jax.experimental.pallas import tpu as pltpu

F32 = jnp.float32
BF16 = jnp.bfloat16

D_MODEL = 1024
HEAD_DIM = 64
GRID_W = 64
EPS = 1e-6
NEG_INF = -1e30
LOG2E = math.log2(math.e)
ROPE_THETA = 10000.0

A_HEADS, A_KV_HEADS, A_HALF_WINDOW = 4, 2, 128
B_PATTERNS = ((128, 1), (512, 4), (2048, 16))
B_GROUPS, B_HEADS = 3, 4
C_HEADS, C_Q_RANK, C_KV_RANK, C_NOPE, C_ROPE, C_V = 4, 256, 128, 64, 32, 64
C_QK_DIM = C_NOPE + C_ROPE
D_HEADS, D_KV_HEADS = 4, 2
N_BRANCHES = 4
REL_BUCKETS, REL_MAX_DIST = 32, 1024
PEER_HEADS, PEER_N_KEYS, PEER_KEY_DIM, PEER_TOPK = 8, 128, 128, 16
PEER_HALF = PEER_KEY_DIM // 2

IN_SPLITS = (256, 128, 128, 768, 768, 768, C_Q_RANK, C_KV_RANK, C_ROPE, 256, 128, 128, N_BRANCHES * D_MODEL)
IN_OFFS = tuple(int(o) for o in np.cumsum((0,) + IN_SPLITS))

LANES = 128
QBLK = 128
VMEM_LIMIT = 56 * 1024 * 1024

P_AQ, P_AK, P_AV = 0, 256, 512
P_BQ, P_BK, P_BV = 768, 1536, 2304
P_CQ, P_CKV, P_CKR = 3072, 3328, 3456
P_DQ, P_DK, P_DV = 3968, 4480, 4736
P_WIDTH = 4992

TM_PROJ = 256
TM_MERGE = 256
BAND_CHUNK = 2048
TQ_FLASH = 512
FLASH_CHUNKS = 4
TM_PREP = 256
TM_PEER = 512
PEER_ROW_TILE = 16
ET_PEER = 1024


def _cparams(sem):
    return pltpu.CompilerParams(dimension_semantics=sem, vmem_limit_bytes=VMEM_LIMIT)


def _full(shape):
    n = len(shape)
    return pl.BlockSpec(shape, lambda *_: (0,) * n)


def _seg_norm(t, bd, gain):
    ms = jnp.dot((t * t).astype(BF16), bd, preferred_element_type=F32)
    return t * lax.rsqrt(ms + EPS) * gain


def _rope128(x, tab_ref):
    up = pltpu.roll(x, LANES - 16, axis=1)
    dn = pltpu.roll(x, 16, axis=1)
    return x * tab_ref[0] + up * tab_ref[1] + dn * tab_ref[2]


def _proj_kernel(h_ref, g1_ref, w_ref, gains_ref, bd64_ref, bd128_ref, bdc_ref, bd256_ref, bdkv_ref,
                 wuq_ref, wukv_ref, gcq_ref, gckn_ref, tabc_ref, tabd_ref,
                 qa_ref, ka_ref, va_ref, qb_ref, kb_ref, vb_ref,
                 qc_ref, kct_ref, vc_ref, qd_ref, kdt_ref, vd_ref):
    x = h_ref[...]
    ms = jnp.mean(x * x, axis=-1, keepdims=True)
    hn = (x * lax.rsqrt(ms + EPS) * g1_ref[...]).astype(BF16)

    def proj(off, width):
        return jnp.dot(hn, w_ref[:, off:off + width], preferred_element_type=F32)

    def gain(off, width):
        return gains_ref[:, off:off + width]

    bd64 = bd64_ref[...]
    bd128 = bd128_ref[...]
    bdc = bdc_ref[...]
    ones_col = gain(P_DV, 256)

    qa_ref[...] = _seg_norm(proj(P_AQ, 256), bd64, gain(P_AQ, 256)).astype(BF16)
    ka_ref[...] = _seg_norm(proj(P_AK, 256), bd64, gain(P_AK, 256)).astype(BF16)
    va_ref[...] = proj(P_AV, 256).astype(BF16)

    for g in range(B_GROUPS):
        o = g * 256
        qb_ref[:, o:o + 256] = _seg_norm(proj(P_BQ + o, 256), bd64, gain(P_BQ + o, 256)).astype(BF16)
        kb_ref[:, o:o + 256] = _seg_norm(proj(P_BK + o, 256), bd64, gain(P_BK + o, 256)).astype(BF16)
        vb_ref[:, o:o + 256] = proj(P_BV + o, 256).astype(BF16)

    cq = _seg_norm(proj(P_CQ, 256), bd256_ref[...], gain(P_CQ, 256)).astype(BF16)
    ckv = _seg_norm(proj(P_CKV, 128), bdkv_ref[...], gain(P_CKV, 128)).astype(BF16)
    for hb in range(2):
        o = hb * 256
        q = jnp.dot(cq, wuq_ref[:, o:o + 256], preferred_element_type=F32)
        q = _seg_norm(q, bdc, gcq_ref[:, o:o + 256])
        kn = jnp.dot(ckv, wukv_ref[:, o:o + 256], preferred_element_type=F32)
        kn = _seg_norm(kn, bd128, gckn_ref[:, o:o + 256])
        kp = _seg_norm(proj(P_CKR + o, 256), bdc, gain(P_CKR + o, 256))
        vc_ref[:, o:o + 256] = (jnp.dot(ckv, wukv_ref[:, 512 + o:512 + o + 256], preferred_element_type=F32)
                                + ones_col).astype(BF16)
        for j in range(2):
            c = j * LANES
            hd = hb * 2 + j
            qc_ref[:, o + c:o + c + LANES] = _rope128(q[:, c:c + LANES], tabc_ref).astype(BF16)
            k = kn[:, c:c + LANES] + _rope128(kp[:, c:c + LANES], tabc_ref)
            kct_ref[hd, 0] = k.T.astype(BF16)

    for hb in range(2):
        o = hb * 256
        q = _seg_norm(proj(P_DQ + o, 256), bd128, gain(P_DQ + o, 256))
        for j in range(2):
            c = j * LANES
            qd_ref[:, o + c:o + c + LANES] = _rope128(q[:, c:c + LANES], tabd_ref).astype(BF16)
    k = _seg_norm(proj(P_DK, 256), bd128, gain(P_DK, 256))
    for j in range(D_KV_HEADS):
        c = j * LANES
        kdt_ref[j, 0] = _rope128(k[:, c:c + LANES], tabd_ref).T.astype(BF16)
    vd_ref[...] = (proj(P_DV, 256) + ones_col).astype(BF16)


def _proj_call(h, lw, consts, tabc, tabd):
    S = h.shape[0]
    tm = TM_PROJ
    nt = S // tm
    row = lambda w: pl.BlockSpec((tm, w), lambda i: (i, 0))
    in_specs = [
        row(D_MODEL), _full((1, D_MODEL)), _full((D_MODEL, P_WIDTH)), _full((1, P_WIDTH)),
        _full((256, 256)), _full((256, 256)), _full((256, 256)), _full((256, 256)), _full((128, 128)),
        _full((C_Q_RANK, 512)), _full((C_KV_RANK, 1024)), _full((1, 512)), _full((1, 512)),
        pl.BlockSpec((3, tm, LANES), lambda i: (0, i, 0)),
        pl.BlockSpec((3, tm, LANES), lambda i: (0, i, 0)),
    ]
    out_shape = [
        jax.ShapeDtypeStruct((S, 256), BF16), jax.ShapeDtypeStruct((S, 256), BF16),
        jax.ShapeDtypeStruct((S, 256), BF16),
        jax.ShapeDtypeStruct((S, 768), BF16), jax.ShapeDtypeStruct((S, 768), BF16),
        jax.ShapeDtypeStruct((S, 768), BF16),
        jax.ShapeDtypeStruct((S, 512), BF16), jax.ShapeDtypeStruct((C_HEADS, nt, LANES, tm), BF16),
        jax.ShapeDtypeStruct((S, 512), BF16),
        jax.ShapeDtypeStruct((S, 512), BF16), jax.ShapeDtypeStruct((D_KV_HEADS, nt, LANES, tm), BF16),
        jax.ShapeDtypeStruct((S, 256), BF16),
    ]
    out_specs = [
        row(256), row(256), row(256), row(768), row(768), row(768),
        row(512), pl.BlockSpec((C_HEADS, 1, LANES, tm), lambda i: (0, i, 0, 0)), row(512),
        row(512), pl.BlockSpec((D_KV_HEADS, 1, LANES, tm), lambda i: (0, i, 0, 0)), row(256),
    ]
    return pl.pallas_call(
        _proj_kernel, grid=(nt,), in_specs=in_specs, out_specs=out_specs, out_shape=out_shape,
        compiler_params=_cparams(("parallel",)), name="proj",
    )(h, lw["g1"], lw["w1"], lw["gains"], consts["bd64"], consts["bd128"], consts["bdc"],
      consts["bd256"], consts["bdkv"], lw["wuq"], lw["wukv"], lw["gcq"], lw["gckn"], tabc, tabd)


def _banded_kernel(*refs, seq_len, has_sink):
    if has_sink:
        sink_ref, refs = refs[0], refs[1:]
    (q_ref, kp_ref, km_ref, kn_ref, vp_ref, vm_ref, vn_ref, bias_ref, o_ref, lse_ref,
     kf, vf, s_buf, p_buf, d_buf, l_buf) = refs
    chunk = q_ref.shape[0]
    nb = chunk // QBLK
    first_blk = pl.program_id(1) * nb
    for dst, parts in ((kf, (kp_ref, km_ref, kn_ref)), (vf, (vp_ref, vm_ref, vn_ref))):
        dst[0:QBLK, :] = parts[0][...]
        dst[QBLK:QBLK + chunk, :] = parts[1][...]
        dst[QBLK + chunk:, :] = parts[2][...]
    lane = lax.broadcasted_iota(jnp.int32, (1, 4 * HEAD_DIM), 1)
    hmasks = [(lane >= h * HEAD_DIM) & (lane < (h + 1) * HEAD_DIM) for h in range(4)]
    key_iota = lax.broadcasted_iota(jnp.int32, (1, 3 * QBLK), 1)

    def blk_rows(j):
        return pl.ds(pl.multiple_of(j * QBLK, QBLK), QBLK)

    def halo_rows(j):
        return pl.ds(pl.multiple_of(j * QBLK, QBLK), 3 * QBLK)

    def scores(j, slot):
        q = q_ref[blk_rows(j), :]
        q4 = jnp.concatenate([jnp.where(hm, q, jnp.zeros_like(q)) for hm in hmasks], axis=0)
        s_buf[slot] = lax.dot_general(q4, kf[halo_rows(j), :], (((1,), (1,)), ((), ())),
                                      preferred_element_type=F32)

    def softmax(j, slot):
        kpos = (first_blk + j - 1) * QBLK + key_iota
        valid = (kpos >= 0) & (kpos < seq_len)
        s = jnp.where(valid, s_buf[slot] + bias_ref[...], NEG_INF)
        m = jnp.max(s, axis=-1, keepdims=True)
        if has_sink:
            m = jnp.maximum(m, sink_ref[...])
        p = jnp.exp(s - m)
        denom = jnp.sum(p, axis=-1, keepdims=True)
        if has_sink:
            denom = denom + jnp.exp(sink_ref[...] - m)
        p_buf[slot] = p.astype(BF16)
        d_buf[slot] = 1.0 / denom
        l_buf[slot] = m + jnp.log(denom)

    def output(j, slot):
        pv = jnp.dot(p_buf[slot], vf[halo_rows(j), :], preferred_element_type=F32) * d_buf[slot]
        lse4 = l_buf[slot]
        out = jnp.zeros((QBLK, 4 * HEAD_DIM), F32)
        lse = jnp.zeros((QBLK, 4 * HEAD_DIM), F32)
        for h in range(4):
            hr = slice(h * QBLK, (h + 1) * QBLK)
            out = out + jnp.where(hmasks[h], pv[hr], 0.0)
            lse = lse + jnp.where(hmasks[h], lse4[hr], 0.0)
        o_ref[blk_rows(j), :] = out.astype(o_ref.dtype)
        lse_ref[blk_rows(j), :] = lse

    scores(0, 0)
    scores(1, 1)
    softmax(0, 0)

    def body(i, carry):
        j = 2 * i
        scores(j, 0)
        softmax(j - 1, 1)
        output(j - 2, 0)
        scores(j + 1, 1)
        softmax(j, 0)
        output(j - 1, 1)
        return carry

    lax.fori_loop(1, nb // 2, body, 0)
    softmax(nb - 1, 1)
    output(nb - 2, 0)
    output(nb - 1, 1)


def _banded_call(q, k, v, bias, sink, dil):
    L = q.shape[0]
    chunk = min(BAND_CHUNK, L)
    nb = chunk // QBLK
    n_blocks = L // QBLK
    assert L % chunk == 0 and nb % 2 == 0 and nb >= 4
    main = pl.BlockSpec((chunk, 256), lambda r, c: (c, r))
    halo = lambda off: pl.BlockSpec((QBLK, 256), lambda r, c: (jnp.clip(c * nb + off, 0, n_blocks - 1), r))
    kv = [halo(-1), main, halo(nb)]
    in_specs = [main] + kv + kv + [_full((4 * QBLK, 3 * QBLK))]
    args = [q, k, k, k, v, v, v, bias.reshape(4 * QBLK, 3 * QBLK)]
    if sink is not None:
        in_specs = [_full((4 * QBLK, 1))] + in_specs
        args = [jnp.repeat(sink.astype(F32), QBLK)[:, None]] + args
    stacked = 4 * QBLK
    return pl.pallas_call(
        functools.partial(_banded_kernel, seq_len=L, has_sink=sink is not None),
        grid=(dil, L // chunk), in_specs=in_specs,
        out_specs=[main, main],
        out_shape=[jax.ShapeDtypeStruct(q.shape, BF16), jax.ShapeDtypeStruct(q.shape, F32)],
        scratch_shapes=[pltpu.VMEM((chunk + 2 * QBLK, 256), BF16), pltpu.VMEM((chunk + 2 * QBLK, 256), BF16),
                        pltpu.VMEM((2, stacked, 3 * QBLK), F32), pltpu.VMEM((2, stacked, 3 * QBLK), BF16),
                        pltpu.VMEM((2, stacked, 1), F32), pltpu.VMEM((2, stacked, 1), F32)],
        compiler_params=_cparams(("parallel", "parallel")), name="banded",
    )(*args)


def _flash_kernel(q_ref, kt_ref, v_ref, o_ref, s_buf, p_buf, a_buf, m_ref, *, n_chunks, chunk, tq):
    per = FLASH_CHUNKS
    n_k = n_chunks // per
    width = per * chunk
    total = (q_ref.shape[0] // tq) * n_k

    def rows(t):
        return pl.ds(pl.multiple_of((t // n_k) * tq, tq), tq)

    def scores(t, slot):
        ks = t % n_k
        kt = jnp.concatenate([kt_ref[ks * per + j] for j in range(per)], axis=1)
        s_buf[slot] = jnp.dot(q_ref[rows(t), :], kt, preferred_element_type=F32)

    def softmax(t, slot):
        s = s_buf[slot]
        m = jnp.where(t % n_k == 0, -jnp.inf, m_ref[...])
        m_new = jnp.maximum(m, jnp.max(s, axis=-1, keepdims=True))
        m_ref[...] = m_new
        a_buf[slot] = jnp.exp2(m - m_new)
        p_buf[slot] = jnp.exp2(s - m_new).astype(BF16)

    def accumulate(t, slot):
        ks = t % n_k
        vv = v_ref[pl.ds(pl.multiple_of(ks * width, width), width), :]
        r = rows(t)
        o_ref[r, :] = a_buf[slot] * o_ref[r, :] + jnp.dot(p_buf[slot], vv, preferred_element_type=F32)

    o_ref[...] = jnp.zeros(o_ref.shape, F32)
    m_ref[...] = jnp.full(m_ref.shape, -jnp.inf, F32)
    scores(0, 0)
    scores(1, 1)
    softmax(0, 0)

    def body(i, carry):
        t = 2 * i
        scores(t, 0)
        softmax(t - 1, 1)
        accumulate(t - 2, 0)
        scores(t + 1, 1)
        softmax(t, 0)
        accumulate(t - 1, 1)
        return carry

    lax.fori_loop(1, total // 2, body, 0)
    softmax(total - 1, 1)
    accumulate(total - 2, 0)
    accumulate(total - 1, 1)


def _flash_call(q, kt, v, n_heads, group):
    S = q.shape[0]
    n_chunks, chunk = kt.shape[1], kt.shape[3]
    tq = min(TQ_FLASH, S)
    width = FLASH_CHUNKS * chunk
    assert n_chunks % (2 * FLASH_CHUNKS) == 0 and S % tq == 0
    once = dict(pipeline_mode=pl.Buffered(1))
    return pl.pallas_call(
        functools.partial(_flash_kernel, n_chunks=n_chunks, chunk=chunk, tq=tq),
        grid=(n_heads,),
        in_specs=[pl.BlockSpec((S, LANES), lambda h: (0, h), **once),
                  pl.BlockSpec((None, n_chunks, LANES, chunk), lambda h: (h // group, 0, 0, 0), **once),
                  pl.BlockSpec((S, LANES), lambda h: (0, h // group), **once)],
        out_specs=pl.BlockSpec((S, LANES), lambda h: (0, h)),
        out_shape=jax.ShapeDtypeStruct((S, n_heads * LANES), F32),
        scratch_shapes=[pltpu.VMEM((2, tq, width), F32), pltpu.VMEM((2, tq, width), BF16),
                        pltpu.VMEM((2, tq, 1), F32), pltpu.VMEM((tq, 1), F32)],
        compiler_params=_cparams(("parallel",)), name="flash",
    )(q, kt, v)


def _merge_kernel(h_ref, g1_ref, wg_ref, gb_ref, ya_ref, ob1_ref, ob2_ref, ob3_ref,
                  ls1_ref, ls2_ref, ls3_ref, yc_ref, yd_ref,
                  wba_ref, wbb_ref, wbc_ref, wbd_ref, wo_ref, o_ref):
    x = h_ref[...]
    ms = jnp.mean(x * x, axis=-1, keepdims=True)
    hn = (x * lax.rsqrt(ms + EPS) * g1_ref[...]).astype(BF16)

    l1, l2, l3 = ls1_ref[...], ls2_ref[...], ls3_ref[...]
    mx = jnp.maximum(jnp.maximum(l1, l2), l3)
    e1, e2, e3 = jnp.exp(l1 - mx), jnp.exp(l2 - mx), jnp.exp(l3 - mx)
    yb = (e1 * ob1_ref[...].astype(F32) + e2 * ob2_ref[...].astype(F32)
          + e3 * ob3_ref[...].astype(F32)) / (e1 + e2 + e3)

    lane = lax.broadcasted_iota(jnp.int32, (1, LANES), 1)

    def normalised(acc_ref):
        heads = []
        for hb in range(acc_ref.shape[1] // LANES):
            blk = acc_ref[:, hb * LANES:(hb + 1) * LANES]
            heads.append(jnp.where(lane < HEAD_DIM, blk / blk[:, HEAD_DIM:HEAD_DIM + 1], 0.0).astype(BF16))
        return jnp.concatenate(heads, axis=1)

    ys = (ya_ref[...], yb.astype(BF16), normalised(yc_ref), normalised(yd_ref))
    wbs = (wba_ref, wbb_ref, wbc_ref, wbd_ref)
    merged = jnp.zeros((x.shape[0], D_MODEL), F32)
    for n in range(N_BRANCHES):
        o = n * D_MODEL
        gpre = jnp.dot(hn, wg_ref[:, o:o + D_MODEL], preferred_element_type=F32) + gb_ref[:, o:o + D_MODEL]
        br = jnp.dot(ys[n], wbs[n][...], preferred_element_type=F32)
        merged = merged + jax.nn.sigmoid(gpre) * br
    o_ref[...] = x + jnp.dot(merged.astype(BF16), wo_ref[...], preferred_element_type=F32)


def _merge_call(h, lw, ya, obs, lses, yc, yd):
    S = h.shape[0]
    tm = TM_MERGE
    row = lambda w: pl.BlockSpec((tm, w), lambda i: (i, 0))
    in_specs = [row(D_MODEL), _full((1, D_MODEL)), _full((D_MODEL, 4 * D_MODEL)), _full((1, 4 * D_MODEL)),
                row(256), row(256), row(256), row(256), row(256), row(256), row(256), row(512), row(512),
                _full((256, D_MODEL)), _full((256, D_MODEL)), _full((512, D_MODEL)), _full((512, D_MODEL)),
                _full((D_MODEL, D_MODEL))]
    return pl.pallas_call(
        _merge_kernel, grid=(S // tm,), in_specs=in_specs, out_specs=row(D_MODEL),
        out_shape=jax.ShapeDtypeStruct((S, D_MODEL), F32),
        compiler_params=_cparams(("parallel",)), name="merge",
    )(h, lw["g1"], lw["wg"], lw["gb"], ya, obs[0], obs[1], obs[2], lses[0], lses[1], lses[2], yc, yd,
      lw["wba"], lw["wbb"], lw["wbc"], lw["wbd"], lw["wo"])


def _top16_desc(s, want_rank=False):
    vals = []
    cur = s
    rank = jnp.full(s.shape, float(PEER_TOPK), F32)
    for k in range(PEER_TOPK):
        m = jnp.max(cur, axis=0, keepdims=True)
        vals.append(m)
        hit = cur == m
        if want_rank:
            rank = jnp.where(hit, float(k), rank)
        cur = jnp.where(hit, -jnp.inf, cur)
    return (vals, rank) if want_rank else vals


def _stack_rows(rows):
    n = len(rows)
    ridx = lax.broadcasted_iota(jnp.int32, (n, rows[0].shape[1]), 0)
    out = jnp.zeros((n, rows[0].shape[1]), F32)
    for r, v in enumerate(rows):
        out = jnp.where(ridx == r, v, out)
    return out


def _dup_bf16(x):
    u = pltpu.bitcast(x.astype(BF16).astype(F32), jnp.uint32)
    return u | (u >> 16)


def _peerprep_kernel(h_ref, g2_ref, wqt_ref, keys_ref, xt_ref, cnt_ref, w1_ref, r2_ref, e2_ref):
    x = h_ref[...]
    ms = jnp.mean(x * x, axis=-1, keepdims=True)
    xn = x * lax.rsqrt(ms + EPS) * g2_ref[...]
    xt = xn.T.astype(BF16)
    xt_ref[...] = xt
    qt = jnp.dot(wqt_ref[...], xt, preferred_element_type=F32).astype(BF16)
    for h in range(PEER_HEADS):
        o = h * PEER_KEY_DIM
        s1 = jnp.dot(keys_ref[h, 0], qt[o:o + PEER_HALF], preferred_element_type=F32)
        s2 = jnp.dot(keys_ref[h, 1], qt[o + PEER_HALF:o + PEER_KEY_DIM], preferred_element_type=F32)
        v1 = _top16_desc(s1)
        v2l, rank2 = _top16_desc(s2, want_rank=True)
        v2 = _stack_rows(v2l)
        cand = jnp.concatenate([v1[0] + v2] + [v1[a] + v2[0:8] for a in range(1, PEER_TOPK)], axis=0)
        top = _top16_desc(cand)
        t0, tau = top[0], top[PEER_TOPK - 1]
        z = jnp.exp(top[0] - t0)
        for t in top[1:]:
            z = z + jnp.exp(t - t0)
        def pick(bits, lo):
            if not bits:
                return v2l[lo]
            (mask, weight), rest = bits[0], bits[1:]
            return jnp.where(mask, pick(rest, lo + weight), pick(rest, lo))

        bits = []
        for weight in (8, 4, 2, 1):
            bits.append((s1 + pick(bits, weight - 1) >= tau, weight))
        cnt = jnp.where(s1 + v2l[PEER_TOPK - 1] >= tau, 1.0, 0.0)
        for mask, weight in bits:
            cnt = cnt + jnp.where(mask, float(weight), 0.0)
        cnt = _dup_bf16(cnt)
        w1 = _dup_bf16(jnp.exp(s1 - v1[0]) / z)
        for c in range(x.shape[0] // LANES):
            cnt_ref[h, c] = cnt[:, c * LANES:(c + 1) * LANES]
            w1_ref[h, c] = w1[:, c * LANES:(c + 1) * LANES]
        r2_ref[h] = rank2.astype(BF16)
        e2_ref[h] = jnp.exp(s2 - v2l[0]).astype(BF16)


def _peerprep_call(h, lw):
    S = h.shape[0]
    tm = min(TM_PREP, S)
    hs = pl.BlockSpec((PEER_HEADS, PEER_N_KEYS, tm), lambda i: (0, 0, i))
    hshape = jax.ShapeDtypeStruct((PEER_HEADS, PEER_N_KEYS, S), BF16)
    cs = pl.BlockSpec((PEER_HEADS, tm // LANES, PEER_N_KEYS, LANES), lambda i: (0, i, 0, 0))
    cshape = jax.ShapeDtypeStruct((PEER_HEADS, S // LANES, PEER_N_KEYS, LANES), jnp.uint32)
    return pl.pallas_call(
        _peerprep_kernel, grid=(S // tm,),
        in_specs=[pl.BlockSpec((tm, D_MODEL), lambda i: (i, 0)), _full((1, D_MODEL)),
                  _full((D_MODEL, D_MODEL)), _full((PEER_HEADS, 2, PEER_N_KEYS, PEER_HALF))],
        out_specs=[pl.BlockSpec((D_MODEL, tm), lambda i: (0, i)), cs, cs, hs, hs],
        out_shape=[jax.ShapeDtypeStruct((D_MODEL, S), BF16), cshape, cshape, hshape, hshape],
        compiler_params=_cparams(("parallel",)), name="peerprep",
    )(h, lw["g2"], lw["wqt"], lw["keys"])


def _replicated_row(ref, h, c, row):
    return ref[h, c, pl.ds(row, 8, stride=0), :]


def _peer_kernel(h_ref, xt_ref, wu_ref, wvt_ref, cnt_ref, w1_ref, r2_in, e2_in, o_ref,
                 acc_ref, a_ref, ag_ref, r2_ref, e2_ref):
    j, e = pl.program_id(0), pl.program_id(1)

    @pl.when((j == 0) & (e == 0))
    def _():
        ag_ref[...] = jnp.zeros_like(ag_ref)
        acc_ref[...] = jnp.zeros_like(acc_ref)

    @pl.when(e == 0)
    def _():
        r2_ref[...] = r2_in[...]
        e2_ref[...] = e2_in[...]

    contrib = jnp.dot(wvt_ref[...], ag_ref[...], preferred_element_type=F32)
    acc_ref[...] = jnp.where(e == 0, 0.0, acc_ref[...] + contrib)

    a_ref[...] = jnp.dot(wu_ref[...], xt_ref[...], preferred_element_type=F32)
    rt = PEER_ROW_TILE
    zero = jnp.zeros((rt, LANES), BF16)
    for il in range(ET_PEER // PEER_N_KEYS):
        for c in range(a_ref.shape[1] // LANES):
            cl = slice(c * LANES, (c + 1) * LANES)
            cnt = [pltpu.bitcast(_replicated_row(cnt_ref, h, c, il), BF16) for h in range(PEER_HEADS)]
            w1 = [pltpu.bitcast(_replicated_row(w1_ref, h, c, il), BF16) for h in range(PEER_HEADS)]
            for r in range(0, PEER_N_KEYS, rt):
                hr = lambda h: slice(h * PEER_N_KEYS + r, h * PEER_N_KEYS + r + rt)
                g = jnp.where(r2_ref[hr(0), cl] < cnt[0], e2_ref[hr(0), cl] * w1[0], zero)
                for h in range(1, PEER_HEADS):
                    g = g + jnp.where(r2_ref[hr(h), cl] < cnt[h], e2_ref[hr(h), cl] * w1[h], zero)
                row = il * PEER_N_KEYS + r
                a = a_ref[row:row + rt, cl]
                act = 0.5 * a * (1.0 + lax.erf(a * (1.0 / math.sqrt(2.0))))
                ag_ref[row:row + rt, cl] = act.astype(BF16) * g

    @pl.when(e == pl.num_programs(1) - 1)
    def _():
        o_ref[...] = h_ref[...] + acc_ref[...].T


def _peer_call(h, xt, cnt, w1, r2, e2, lw):
    S = h.shape[0]
    tm = min(TM_PEER, S)
    n_exp = lw["wu"].shape[0]
    rows = ET_PEER // PEER_N_KEYS
    n_tiles = n_exp // ET_PEER
    cur = lambda e: jnp.minimum(e, n_tiles - 1)
    prev = lambda e: jnp.maximum(e - 1, 0)
    tile = pl.BlockSpec((PEER_HEADS, tm // LANES, rows, LANES), lambda j, e: (0, j, cur(e), 0))
    res = pl.BlockSpec((PEER_HEADS * PEER_N_KEYS, tm), lambda j, e: (0, j))
    return pl.pallas_call(
        _peer_kernel, grid=(S // tm, n_tiles + 1),
        in_specs=[pl.BlockSpec((tm, D_MODEL), lambda j, e: (j, 0)),
                  pl.BlockSpec((D_MODEL, tm), lambda j, e: (0, j)),
                  pl.BlockSpec((ET_PEER, D_MODEL), lambda j, e: (cur(e), 0)),
                  pl.BlockSpec((None, D_MODEL, ET_PEER), lambda j, e: (prev(e), 0, 0)),
                  tile, tile, res, res],
        out_specs=pl.BlockSpec((tm, D_MODEL), lambda j, e: (j, 0)),
        out_shape=jax.ShapeDtypeStruct((S, D_MODEL), F32),
        scratch_shapes=[pltpu.VMEM((D_MODEL, tm), F32), pltpu.VMEM((ET_PEER, tm), F32),
                        pltpu.VMEM((ET_PEER, tm), BF16),
                        pltpu.VMEM((PEER_HEADS * PEER_N_KEYS, tm), BF16),
                        pltpu.VMEM((PEER_HEADS * PEER_N_KEYS, tm), BF16)],
        compiler_params=_cparams(("arbitrary", "arbitrary")), name="peer",
    )(h, xt, lw["wu"], lw["wvt"], cnt, w1, r2.reshape(-1, S), e2.reshape(-1, S))


def _pad_heads(w, n_heads, width):
    lead = w.shape[:-1]
    w = w.reshape(*lead, n_heads, width)
    w = jnp.pad(w, [(0, 0)] * len(lead) + [(0, 0), (0, LANES - width)])
    return w.reshape(*lead, n_heads * LANES)


def _dup_kv(w):
    lead = w.shape[:-1]
    w = w.reshape(*lead, A_KV_HEADS, 1, HEAD_DIM)
    w = jnp.broadcast_to(w, (*lead, A_KV_HEADS, A_HEADS // A_KV_HEADS, HEAD_DIM))
    return w.reshape(*lead, A_HEADS * HEAD_DIM)


def _block_diag(width, segments):
    m = np.zeros((width, width), np.float32)
    for a, b, d in segments:
        m[a:b, a:b] = 1.0 / d
    return jnp.asarray(m, BF16)


def _consts():
    seg64 = [(i * 64, (i + 1) * 64, 64) for i in range(4)]
    seg128 = [(i * 128, (i + 1) * 128, 64) for i in range(2)]
    segc = []
    for i in range(2):
        segc += [(i * 128, i * 128 + 64, 64), (i * 128 + 64, (i + 1) * 128, 32)]
    return {
        "bd64": _block_diag(256, seg64), "bd128": _block_diag(256, seg128), "bdc": _block_diag(256, segc),
        "bd256": _block_diag(256, [(0, 256, 256)]), "bdkv": _block_diag(128, [(0, 128, 128)]),
    }


def _layer_weights(l, p):
    w = p["w_in"][l]
    cols = [w[:, IN_OFFS[i]:IN_OFFS[i + 1]] for i in range(len(IN_SPLITS))]
    aq, ak, av, bq, bk, bv, cq, ckv, ckr, dq, dk, dv, wgate = cols
    ckr_rep = jnp.zeros((D_MODEL, C_HEADS, LANES), F32).at[:, :, C_NOPE:C_QK_DIM].set(ckr[:, None, :])
    w1 = jnp.concatenate([
        aq, _dup_kv(ak), _dup_kv(av), bq, bk, bv, cq, ckv, ckr_rep.reshape(D_MODEL, C_HEADS * LANES),
        _pad_heads(dq, D_HEADS, HEAD_DIM), _pad_heads(dk, D_KV_HEADS, HEAD_DIM),
        _pad_heads(dv, D_KV_HEADS, HEAD_DIM)], axis=1).astype(BF16)

    scale = HEAD_DIM ** -0.5
    ones = lambda n: jnp.ones((n,), F32)
    ones_col = jnp.zeros((2, LANES), F32).at[:, HEAD_DIM].set(1.0).reshape(-1)
    ck_gain = p["c_k_gain"][l]
    ckr_gain = jnp.zeros((C_HEADS, LANES), F32).at[:, C_NOPE:C_QK_DIM].set(ck_gain[None, C_NOPE:])
    gains = jnp.concatenate([
        jnp.tile(p["a_q_gain"][l], 4) * scale, jnp.tile(p["a_k_gain"][l], 4), ones(256),
        jnp.tile(p["b_q_gain"][l], (1, 4)).reshape(-1) * scale, jnp.tile(p["b_k_gain"][l], (1, 4)).reshape(-1),
        ones(768),
        p["c_q_lat_gain"][l], p["c_kv_lat_gain"][l], ckr_gain.reshape(-1),
        jnp.tile(_pad_heads(p["d_q_gain"][l], 1, HEAD_DIM), 4) * (scale * LOG2E),
        jnp.tile(_pad_heads(p["d_k_gain"][l], 1, HEAD_DIM), 2), ones_col])[None, :]

    cq_gain = jnp.tile(_pad_heads(p["c_q_gain"][l], 1, C_QK_DIM), C_HEADS) * (C_QK_DIM ** -0.5 * LOG2E)
    ckn_gain = jnp.tile(_pad_heads(ck_gain[:C_NOPE], 1, C_NOPE), C_HEADS)
    wukv = p["c_w_ukv"][l].reshape(C_KV_RANK, C_HEADS, C_NOPE + C_V)
    wukv = jnp.concatenate([_pad_heads(wukv[:, :, :C_NOPE].reshape(C_KV_RANK, -1), C_HEADS, C_NOPE),
                            _pad_heads(wukv[:, :, C_NOPE:].reshape(C_KV_RANK, -1), C_HEADS, C_V)], axis=1)

    wb = p["w_branch"][l]
    pad_rows = lambda m: _pad_heads(m.T, 4, HEAD_DIM).T
    return {
        "g1": p["norm1_gain"][l][None, :], "w1": w1, "gains": gains,
        "wuq": _pad_heads(p["c_w_uq"][l], C_HEADS, C_QK_DIM).astype(BF16), "wukv": wukv.astype(BF16),
        "gcq": cq_gain[None, :], "gckn": ckn_gain[None, :],
        "wg": wgate.astype(BF16), "gb": p["gate_bias"][l][None, :],
        "wba": wb[0].astype(BF16), "wbb": wb[1].astype(BF16),
        "wbc": pad_rows(wb[2]).astype(BF16), "wbd": pad_rows(wb[3]).astype(BF16),
        "wo": p["w_out"][l].astype(BF16),
        "g2": p["norm2_gain"][l][None, :], "wqt": p["peer_w_q"][l].T.astype(BF16),
        "keys": p["peer_sub_keys"][l].astype(BF16),
        "wu": p["peer_w_u"][l].astype(BF16),
        "wvt": p["peer_w_v"][l].reshape(-1, ET_PEER, D_MODEL).transpose(0, 2, 1).astype(BF16),
    }


def _rope_tables(S):
    half = 16
    freqs = ROPE_THETA ** (-jnp.arange(half, dtype=F32) / half)

    def cs(pos):
        ang = pos.astype(F32)[:, None] * freqs[None, :]
        return jnp.cos(ang), jnp.sin(ang)

    def table(parts):
        cos, sa, sb = [], [], []
        at = 0
        for start, (c, s) in parts:
            z = jnp.zeros((S, start - at), F32)
            zh = jnp.zeros((S, half), F32)
            cos += [z + 1.0, c, c]
            sa += [z, -s, zh]
            sb += [z, zh, s]
            at = start + 2 * half
        z = jnp.zeros((S, LANES - at), F32)
        cat = lambda pieces: jnp.concatenate(pieces, axis=1)
        return jnp.stack([cat(cos + [z + 1.0]), cat(sa + [z]), cat(sb + [z])])

    pos = jnp.arange(S, dtype=jnp.int32)
    tabc = table([(C_NOPE, cs(pos))])
    n_rows = S // GRID_W
    by_row = [jnp.repeat(t, GRID_W, axis=0) for t in cs(jnp.arange(n_rows, dtype=jnp.int32))]
    by_col = [jnp.tile(t, (n_rows, 1)) for t in cs(jnp.arange(GRID_W, dtype=jnp.int32))]
    tabd = table([(0, by_row), (2 * half, by_col)])
    return tabc, tabd


def _rel_bucket(rel):
    nb = REL_BUCKETS // 2
    max_exact = nb // 2
    ret = jnp.where(rel > 0, nb, 0)
    n = jnp.abs(rel)
    nf = jnp.maximum(n, 1).astype(F32)
    large = max_exact + (jnp.log(nf / max_exact) / math.log(REL_MAX_DIST / max_exact)
                         * (nb - max_exact)).astype(jnp.int32)
    large = jnp.minimum(large, nb - 1)
    return ret + jnp.where(n < max_exact, n, large)


def _band_bias(rel_table, half, dil):
    rel = jnp.arange(3 * QBLK)[None, :] - QBLK - jnp.arange(QBLK)[:, None]
    onehot = (_rel_bucket(rel * dil)[None, :, :] == jnp.arange(REL_BUCKETS)[:, None, None]).astype(F32)
    bias = jnp.einsum("bh,bqs->hqs", rel_table.astype(F32), onehot, precision=lax.Precision.HIGHEST)
    return jnp.where((jnp.abs(rel) <= half)[None], bias, NEG_INF)


def kernel(x, rel_bias, norm1_gain, w_in, gate_bias, a_q_gain, a_k_gain, a_sink, b_q_gain, b_k_gain, c_q_lat_gain, c_kv_lat_gain, c_w_uq, c_w_ukv, c_q_gain, c_k_gain, d_q_gain, d_k_gain, w_branch, w_out, norm2_gain, peer_w_q, peer_sub_keys, peer_w_u, peer_w_v):
    p = dict(rel_bias=rel_bias, norm1_gain=norm1_gain, w_in=w_in, gate_bias=gate_bias, a_q_gain=a_q_gain,
             a_k_gain=a_k_gain, a_sink=a_sink, b_q_gain=b_q_gain, b_k_gain=b_k_gain,
             c_q_lat_gain=c_q_lat_gain, c_kv_lat_gain=c_kv_lat_gain, c_w_uq=c_w_uq, c_w_ukv=c_w_ukv,
             c_q_gain=c_q_gain, c_k_gain=c_k_gain, d_q_gain=d_q_gain, d_k_gain=d_k_gain, w_branch=w_branch,
             w_out=w_out, norm2_gain=norm2_gain, peer_w_q=peer_w_q, peer_sub_keys=peer_sub_keys,
             peer_w_u=peer_w_u, peer_w_v=peer_w_v)
    Bn, S, D = x.shape
    assert Bn == 1 and D == D_MODEL and S % (16 * QBLK) == 0
    depth = w_in.shape[0]
    consts = _consts()
    tabc, tabd = _rope_tables(S)
    bias_a = _band_bias(rel_bias[:, :A_HEADS], A_HALF_WINDOW, 1)
    bias_b = [_band_bias(rel_bias[:, A_HEADS + g * B_HEADS:A_HEADS + (g + 1) * B_HEADS], win // (2 * dil), dil)
              for g, (win, dil) in enumerate(B_PATTERNS)]

    h = x.reshape(S, D)
    for l in range(depth):
        lw = _layer_weights(l, p)
        qa, ka, va, qb, kb, vb, qc, kct, vc, qd, kdt, vd = _proj_call(h, lw, consts, tabc, tabd)
        ya, _ = _banded_call(qa, ka, va, bias_a, a_sink[l], 1)
        obs, lses = [], []
        for g, (win, dil) in enumerate(B_PATTERNS):
            view = lambda t: t[:, g * 256:(g + 1) * 256].reshape(S // dil, dil * 256)
            o, lse = _banded_call(view(qb), view(kb), view(vb), bias_b[g], None, dil)
            obs.append(o.reshape(S, 256))
            lses.append(lse.reshape(S, 256))
        yc = _flash_call(qc, kct, vc, C_HEADS, 1)
        yd = _flash_call(qd, kdt, vd, D_HEADS, D_HEADS // D_KV_HEADS)
        h = _merge_call(h, lw, ya, obs, lses, yc, yd)
        xt, cnt, w1, r2, e2 = _peerprep_call(h, lw)
        h = _peer_call(h, xt, cnt, w1, r2, e2, lw)
    return h.reshape(Bn, S, D)
```

```python
import functools
import math

import jax
import jax.numpy as jnp
import numpy as np
from jax import lax
from jax.experimental import pallas as pl
from jax.experimental.pallas import tpu as pltpu

F32 = jnp.float32
BF16 = jnp.bfloat16

D_MODEL = 1024
HEAD_DIM = 64
GRID_W = 64
EPS = 1e-6
NEG_INF = -1e30
LOG2E = math.log2(math.e)
ROPE_THETA = 10000.0

A_HEADS, A_KV_HEADS, A_HALF_WINDOW = 4, 2, 128
B_PATTERNS = ((128, 1), (512, 4), (2048, 16))
B_GROUPS, B_HEADS = 3, 4
C_HEADS, C_Q_RANK, C_KV_RANK, C_NOPE, C_ROPE, C_V = 4, 256, 128, 64, 32, 64
C_QK_DIM = C_NOPE + C_ROPE
D_HEADS, D_KV_HEADS = 4, 2
N_BRANCHES = 4
REL_BUCKETS, REL_MAX_DIST = 32, 1024
PEER_HEADS, PEER_N_KEYS, PEER_KEY_DIM, PEER_TOPK = 8, 128, 128, 16
PEER_HALF = PEER_KEY_DIM // 2

IN_SPLITS = (256, 128, 128, 768, 768, 768, C_Q_RANK, C_KV_RANK, C_ROPE, 256, 128, 128, N_BRANCHES * D_MODEL)
IN_OFFS = tuple(int(o) for o in np.cumsum((0,) + IN_SPLITS))

LANES = 128
QBLK = 128
VMEM_LIMIT = 56 * 1024 * 1024

P_AQ, P_AK, P_AV = 0, 256, 512
P_BQ, P_BK, P_BV = 768, 1536, 2304
P_CQ, P_CKV, P_CKR = 3072, 3328, 3456
P_DQ, P_DK, P_DV = 3968, 4480, 4736
P_WIDTH = 4992

TM_PROJ = 256
TM_MERGE = 256
BAND_CHUNK = 2048
TQ_FLASH = 512
FLASH_CHUNKS = 4
TM_PREP = 256
TM_PEER = 512
PEER_ROW_TILE = 16
ET_PEER = 1024


def _cparams(sem):
    return pltpu.CompilerParams(dimension_semantics=sem, vmem_limit_bytes=VMEM_LIMIT)


def _full(shape):
    n = len(shape)
    return pl.BlockSpec(shape, lambda *_: (0,) * n)


def _seg_norm(t, bd, gain):
    ms = jnp.dot((t * t).astype(BF16), bd, preferred_element_type=F32)
    return t * lax.rsqrt(ms + EPS) * gain


def _rope128(x, tab_ref):
    up = pltpu.roll(x, LANES - 16, axis=1)
    dn = pltpu.roll(x, 16, axis=1)
    return x * tab_ref[0] + up * tab_ref[1] + dn * tab_ref[2]


def _proj_kernel(h_ref, g1_ref, w_ref, gains_ref, bd64_ref, bd128_ref, bdc_ref, bd256_ref, bdkv_ref,
                 wuq_ref, wukv_ref, gcq_ref, gckn_ref, tabc_ref, tabd_ref,
                 qa_ref, ka_ref, va_ref, qb_ref, kb_ref, vb_ref,
                 qc_ref, kct_ref, vc_ref, qd_ref, kdt_ref, vd_ref):
    x = h_ref[...]
    ms = jnp.mean(x * x, axis=-1, keepdims=True)
    hn = (x * lax.rsqrt(ms + EPS) * g1_ref[...]).astype(BF16)

    def proj(off, width):
        return jnp.dot(hn, w_ref[:, off:off + width], preferred_element_type=F32)

    def gain(off, width):
        return gains_ref[:, off:off + width]

    bd64 = bd64_ref[...]
    bd128 = bd128_ref[...]
    bdc = bdc_ref[...]
    ones_col = gain(P_DV, 256)

    qa_ref[...] = _seg_norm(proj(P_AQ, 256), bd64, gain(P_AQ, 256)).astype(BF16)
    ka_ref[...] = _seg_norm(proj(P_AK, 256), bd64, gain(P_AK, 256)).astype(BF16)
    va_ref[...] = proj(P_AV, 256).astype(BF16)

    for g in range(B_GROUPS):
        o = g * 256
        qb_ref[:, o:o + 256] = _seg_norm(proj(P_BQ + o, 256), bd64, gain(P_BQ + o, 256)).astype(BF16)
        kb_ref[:, o:o + 256] = _seg_norm(proj(P_BK + o, 256), bd64, gain(P_BK + o, 256)).astype(BF16)
        vb_ref[:, o:o + 256] = proj(P_BV + o, 256).astype(BF16)

    cq = _seg_norm(proj(P_CQ, 256), bd256_ref[...], gain(P_CQ, 256)).astype(BF16)
    ckv = _seg_norm(proj(P_CKV, 128), bdkv_ref[...], gain(P_CKV, 128)).astype(BF16)
    for hb in range(2):
        o = hb * 256
        q = jnp.dot(cq, wuq_ref[:, o:o + 256], preferred_element_type=F32)
        q = _seg_norm(q, bdc, gcq_ref[:, o:o + 256])
        kn = jnp.dot(ckv, wukv_ref[:, o:o + 256], preferred_element_type=F32)
        kn = _seg_norm(kn, bd128, gckn_ref[:, o:o + 256])
        kp = _seg_norm(proj(P_CKR + o, 256), bdc, gain(P_CKR + o, 256))
        vc_ref[:, o:o + 256] = (jnp.dot(ckv, wukv_ref[:, 512 + o:512 + o + 256], preferred_element_type=F32)
                                + ones_col).astype(BF16)
        for j in range(2):
            c = j * LANES
            hd = hb * 2 + j
            qc_ref[:, o + c:o + c + LANES] = _rope128(q[:, c:c + LANES], tabc_ref).astype(BF16)
            k = kn[:, c:c + LANES] + _rope128(kp[:, c:c + LANES], tabc_ref)
            kct_ref[hd, 0] = k.T.astype(BF16)

    for hb in range(2):
        o = hb * 256
        q = _seg_norm(proj(P_DQ + o, 256), bd128, gain(P_DQ + o, 256))
        for j in range(2):
            c = j * LANES
            qd_ref[:, o + c:o + c + LANES] = _rope128(q[:, c:c + LANES], tabd_ref).astype(BF16)
    k = _seg_norm(proj(P_DK, 256), bd128, gain(P_DK, 256))
    for j in range(D_KV_HEADS):
        c = j * LANES
        kdt_ref[j, 0] = _rope128(k[:, c:c + LANES], tabd_ref).T.astype(BF16)
    vd_ref[...] = (proj(P_DV, 256) + ones_col).astype(BF16)


def _proj_call(h, lw, consts, tabc, tabd):
    S = h.shape[0]
    tm = TM_PROJ
    nt = S // tm
    row = lambda w: pl.BlockSpec((tm, w), lambda i: (i, 0))
    in_specs = [
        row(D_MODEL), _full((1, D_MODEL)), _full((D_MODEL, P_WIDTH)), _full((1, P_WIDTH)),
        _full((256, 256)), _full((256, 256)), _full((256, 256)), _full((256, 256)), _full((128, 128)),
        _full((C_Q_RANK, 512)), _full((C_KV_RANK, 1024)), _full((1, 512)), _full((1, 512)),
        pl.BlockSpec((3, tm, LANES), lambda i: (0, i, 0)),
        pl.BlockSpec((3, tm, LANES), lambda i: (0, i, 0)),
    ]
    out_shape = [
        jax.ShapeDtypeStruct((S, 256), BF16), jax.ShapeDtypeStruct((S, 256), BF16),
        jax.ShapeDtypeStruct((S, 256), BF16),
        jax.ShapeDtypeStruct((S, 768), BF16), jax.ShapeDtypeStruct((S, 768), BF16),
        jax.ShapeDtypeStruct((S, 768), BF16),
        jax.ShapeDtypeStruct((S, 512), BF16), jax.ShapeDtypeStruct((C_HEADS, nt, LANES, tm), BF16),
        jax.ShapeDtypeStruct((S, 512), BF16),
        jax.ShapeDtypeStruct((S, 512), BF16), jax.ShapeDtypeStruct((D_KV_HEADS, nt, LANES, tm), BF16),
        jax.ShapeDtypeStruct((S, 256), BF16),
    ]
    out_specs = [
        row(256), row(256), row(256), row(768), row(768), row(768),
        row(512), pl.BlockSpec((C_HEADS, 1, LANES, tm), lambda i: (0, i, 0, 0)), row(512),
        row(512), pl.BlockSpec((D_KV_HEADS, 1, LANES, tm), lambda i: (0, i, 0, 0)), row(256),
    ]
    return pl.pallas_call(
        _proj_kernel, grid=(nt,), in_specs=in_specs, out_specs=out_specs, out_shape=out_shape,
        compiler_params=_cparams(("parallel",)), name="proj",
    )(h, lw["g1"], lw["w1"], lw["gains"], consts["bd64"], consts["bd128"], consts["bdc"],
      consts["bd256"], consts["bdkv"], lw["wuq"], lw["wukv"], lw["gcq"], lw["gckn"], tabc, tabd)


def _banded_kernel(*refs, seq_len, has_sink):
    if has_sink:
        sink_ref, refs = refs[0], refs[1:]
    (q_ref, kp_ref, km_ref, kn_ref, vp_ref, vm_ref, vn_ref, bias_ref, o_ref, lse_ref,
     kf, vf, s_buf, p_buf, d_buf, l_buf) = refs
    chunk = q_ref.shape[0]
    nb = chunk // QBLK
    first_blk = pl.program_id(1) * nb
    for dst, parts in ((kf, (kp_ref, km_ref, kn_ref)), (vf, (vp_ref, vm_ref, vn_ref))):
        dst[0:QBLK, :] = parts[0][...]
        dst[QBLK:QBLK + chunk, :] = parts[1][...]
        dst[QBLK + chunk:, :] = parts[2][...]
    lane = lax.broadcasted_iota(jnp.int32, (1, 4 * HEAD_DIM), 1)
    hmasks = [(lane >= h * HEAD_DIM) & (lane < (h + 1) * HEAD_DIM) for h in range(4)]
    key_iota = lax.broadcasted_iota(jnp.int32, (1, 3 * QBLK), 1)

    def blk_rows(j):
        return pl.ds(pl.multiple_of(j * QBLK, QBLK), QBLK)

    def halo_rows(j):
        return pl.ds(pl.multiple_of(j * QBLK, QBLK), 3 * QBLK)

    def scores(j, slot):
        q = q_ref[blk_rows(j), :]
        q4 = jnp.concatenate([jnp.where(hm, q, jnp.zeros_like(q)) for hm in hmasks], axis=0)
        s_buf[slot] = lax.dot_general(q4, kf[halo_rows(j), :], (((1,), (1,)), ((), ())),
                                      preferred_element_type=F32)

    def softmax(j, slot):
        kpos = (first_blk + j - 1) * QBLK + key_iota
        valid = (kpos >= 0) & (kpos < seq_len)
        s = jnp.where(valid, s_buf[slot] + bias_ref[...], NEG_INF)
        m = jnp.max(s, axis=-1, keepdims=True)
        if has_sink:
            m = jnp.maximum(m, sink_ref[...])
        p = jnp.exp(s - m)
        denom = jnp.sum(p, axis=-1, keepdims=True)
        if has_sink:
            denom = denom + jnp.exp(sink_ref[...] - m)
        p_buf[slot] = p.astype(BF16)
        d_buf[slot] = 1.0 / denom
        l_buf[slot] = m + jnp.log(denom)

    def output(j, slot):
        pv = jnp.dot(p_buf[slot], vf[halo_rows(j), :], preferred_element_type=F32) * d_buf[slot]
        lse4 = l_buf[slot]
        out = jnp.zeros((QBLK, 4 * HEAD_DIM), F32)
        lse = jnp.zeros((QBLK, 4 * HEAD_DIM), F32)
        for h in range(4):
            hr = slice(h * QBLK, (h + 1) * QBLK)
            out = out + jnp.where(hmasks[h], pv[hr], 0.0)
            lse = lse + jnp.where(hmasks[h], lse4[hr], 0.0)
        o_ref[blk_rows(j), :] = out.astype(o_ref.dtype)
        lse_ref[blk_rows(j), :] = lse

    scores(0, 0)
    scores(1, 1)
    softmax(0, 0)

    def body(i, carry):
        j = 2 * i
        scores(j, 0)
        softmax(j - 1, 1)
        output(j - 2, 0)
        scores(j + 1, 1)
        softmax(j, 0)
        output(j - 1, 1)
        return carry

    lax.fori_loop(1, nb // 2, body, 0)
    softmax(nb - 1, 1)
    output(nb - 2, 0)
    output(nb - 1, 1)


def _banded_call(q, k, v, bias, sink, dil):
    L = q.shape[0]
    chunk = min(BAND_CHUNK, L)
    nb = chunk // QBLK
    n_blocks = L // QBLK
    assert L % chunk == 0 and nb % 2 == 0 and nb >= 4
    main = pl.BlockSpec((chunk, 256), lambda r, c: (c, r))
    halo = lambda off: pl.BlockSpec((QBLK, 256), lambda r, c: (jnp.clip(c * nb + off, 0, n_blocks - 1), r))
    kv = [halo(-1), main, halo(nb)]
    in_specs = [main] + kv + kv + [_full((4 * QBLK, 3 * QBLK))]
    args = [q, k, k, k, v, v, v, bias.reshape(4 * QBLK, 3 * QBLK)]
    if sink is not None:
        in_specs = [_full((4 * QBLK, 1))] + in_specs
        args = [jnp.repeat(sink.astype(F32), QBLK)[:, None]] + args
    stacked = 4 * QBLK
    return pl.pallas_call(
        functools.partial(_banded_kernel, seq_len=L, has_sink=sink is not None),
        grid=(dil, L // chunk), in_specs=in_specs,
        out_specs=[main, main],
        out_shape=[jax.ShapeDtypeStruct(q.shape, BF16), jax.ShapeDtypeStruct(q.shape, F32)],
        scratch_shapes=[pltpu.VMEM((chunk + 2 * QBLK, 256), BF16), pltpu.VMEM((chunk + 2 * QBLK, 256), BF16),
                        pltpu.VMEM((2, stacked, 3 * QBLK), F32), pltpu.VMEM((2, stacked, 3 * QBLK), BF16),
                        pltpu.VMEM((2, stacked, 1), F32), pltpu.VMEM((2, stacked, 1), F32)],
        compiler_params=_cparams(("parallel", "parallel")), name="banded",
    )(*args)


def _flash_kernel(q_ref, kt_ref, v_ref, o_ref, s_buf, p_buf, a_buf, m_ref, *, n_chunks, chunk, tq):
    per = FLASH_CHUNKS
    n_k = n_chunks // per
    width = per * chunk
    total = (q_ref.shape[0] // tq) * n_k

    def rows(t):
        return pl.ds(pl.multiple_of((t // n_k) * tq, tq), tq)

    def scores(t, slot):
        ks = t % n_k
        kt = jnp.concatenate([kt_ref[ks * per + j] for j in range(per)], axis=1)
        s_buf[slot] = jnp.dot(q_ref[rows(t), :], kt, preferred_element_type=F32)

    def softmax(t, slot):
        s = s_buf[slot]
        m = jnp.where(t % n_k == 0, -jnp.inf, m_ref[...])
        m_new = jnp.maximum(m, jnp.max(s, axis=-1, keepdims=True))
        m_ref[...] = m_new
        a_buf[slot] = jnp.exp2(m - m_new)
        p_buf[slot] = jnp.exp2(s - m_new).astype(BF16)

    def accumulate(t, slot):
        ks = t % n_k
        vv = v_ref[pl.ds(pl.multiple_of(ks * width, width), width), :]
        r = rows(t)
        o_ref[r, :] = a_buf[slot] * o_ref[r, :] + jnp.dot(p_buf[slot], vv, preferred_element_type=F32)

    o_ref[...] = jnp.zeros(o_ref.shape, F32)
    m_ref[...] = jnp.full(m_ref.shape, -jnp.inf, F32)
    scores(0, 0)
    scores(1, 1)
    softmax(0, 0)

    def body(i, carry):
        t = 2 * i
        scores(t, 0)
        softmax(t - 1, 1)
        accumulate(t - 2, 0)
        scores(t + 1, 1)
        softmax(t, 0)
        accumulate(t - 1, 1)
        return carry

    lax.fori_loop(1, total // 2, body, 0)
    softmax(total - 1, 1)
    accumulate(total - 2, 0)
    accumulate(total - 1, 1)


def _flash_call(q, kt, v, n_heads, group):
    S = q.shape[0]
    n_chunks, chunk = kt.shape[1], kt.shape[3]
    tq = min(TQ_FLASH, S)
    width = FLASH_CHUNKS * chunk
    assert n_chunks % (2 * FLASH_CHUNKS) == 0 and S % tq == 0
    once = dict(pipeline_mode=pl.Buffered(1))
    return pl.pallas_call(
        functools.partial(_flash_kernel, n_chunks=n_chunks, chunk=chunk, tq=tq),
        grid=(n_heads,),
        in_specs=[pl.BlockSpec((S, LANES), lambda h: (0, h), **once),
                  pl.BlockSpec((None, n_chunks, LANES, chunk), lambda h: (h // group, 0, 0, 0), **once),
                  pl.BlockSpec((S, LANES), lambda h: (0, h // group), **once)],
        out_specs=pl.BlockSpec((S, LANES), lambda h: (0, h)),
        out_shape=jax.ShapeDtypeStruct((S, n_heads * LANES), F32),
        scratch_shapes=[pltpu.VMEM((2, tq, width), F32), pltpu.VMEM((2, tq, width), BF16),
                        pltpu.VMEM((2, tq, 1), F32), pltpu.VMEM((tq, 1), F32)],
        compiler_params=_cparams(("parallel",)), name="flash",
    )(q, kt, v)


def _merge_kernel(h_ref, g1_ref, wg_ref, gb_ref, ya_ref, ob1_ref, ob2_ref, ob3_ref,
                  ls1_ref, ls2_ref, ls3_ref, yc_ref, yd_ref,
                  wba_ref, wbb_ref, wbc_ref, wbd_ref, wo_ref, o_ref):
    x = h_ref[...]
    ms = jnp.mean(x * x, axis=-1, keepdims=True)
    hn = (x * lax.rsqrt(ms + EPS) * g1_ref[...]).astype(BF16)

    l1, l2, l3 = ls1_ref[...], ls2_ref[...], ls3_ref[...]
    mx = jnp.maximum(jnp.maximum(l1, l2), l3)
    e1, e2, e3 = jnp.exp(l1 - mx), jnp.exp(l2 - mx), jnp.exp(l3 - mx)
    yb = (e1 * ob1_ref[...].astype(F32) + e2 * ob2_ref[...].astype(F32)
          + e3 * ob3_ref[...].astype(F32)) / (e1 + e2 + e3)

    lane = lax.broadcasted_iota(jnp.int32, (1, LANES), 1)

    def normalised(acc_ref):
        heads = []
        for hb in range(acc_ref.shape[1] // LANES):
            blk = acc_ref[:, hb * LANES:(hb + 1) * LANES]
            heads.append(jnp.where(lane < HEAD_DIM, blk / blk[:, HEAD_DIM:HEAD_DIM + 1], 0.0).astype(BF16))
        return jnp.concatenate(heads, axis=1)

    ys = (ya_ref[...], yb.astype(BF16), normalised(yc_ref), normalised(yd_ref))
    wbs = (wba_ref, wbb_ref, wbc_ref, wbd_ref)
    merged = jnp.zeros((x.shape[0], D_MODEL), F32)
    for n in range(N_BRANCHES):
        o = n * D_MODEL
        gpre = jnp.dot(hn, wg_ref[:, o:o + D_MODEL], preferred_element_type=F32) + gb_ref[:, o:o + D_MODEL]
        br = jnp.dot(ys[n], wbs[n][...], preferred_element_type=F32)
        merged = merged + jax.nn.sigmoid(gpre) * br
    o_ref[...] = x + jnp.dot(merged.astype(BF16), wo_ref[...], preferred_element_type=F32)


def _merge_call(h, lw, ya, obs, lses, yc, yd):
    S = h.shape[0]
    tm = TM_MERGE
    row = lambda w: pl.BlockSpec((tm, w), lambda i: (i, 0))
    in_specs = [row(D_MODEL), _full((1, D_MODEL)), _full((D_MODEL, 4 * D_MODEL)), _full((1, 4 * D_MODEL)),
                row(256), row(256), row(256), row(256), row(256), row(256), row(256), row(512), row(512),
                _full((256, D_MODEL)), _full((256, D_MODEL)), _full((512, D_MODEL)), _full((512, D_MODEL)),
                _full((D_MODEL, D_MODEL))]
    return pl.pallas_call(
        _merge_kernel, grid=(S // tm,), in_specs=in_specs, out_specs=row(D_MODEL),
        out_shape=jax.ShapeDtypeStruct((S, D_MODEL), F32),
        compiler_params=_cparams(("parallel",)), name="merge",
    )(h, lw["g1"], lw["wg"], lw["gb"], ya, obs[0], obs[1], obs[2], lses[0], lses[1], lses[2], yc, yd,
      lw["wba"], lw["wbb"], lw["wbc"], lw["wbd"], lw["wo"])


def _top16_desc(s, want_rank=False):
    vals = []
    cur = s
    rank = jnp.full(s.shape, float(PEER_TOPK), F32)
    for k in range(PEER_TOPK):
        m = jnp.max(cur, axis=0, keepdims=True)
        vals.append(m)
        hit = cur == m
        if want_rank:
            rank = jnp.where(hit, float(k), rank)
        cur = jnp.where(hit, -jnp.inf, cur)
    return (vals, rank) if want_rank else vals


def _stack_rows(rows):
    n = len(rows)
    ridx = lax.broadcasted_iota(jnp.int32, (n, rows[0].shape[1]), 0)
    out = jnp.zeros((n, rows[0].shape[1]), F32)
    for r, v in enumerate(rows):
        out = jnp.where(ridx == r, v, out)
    return out


def _dup_bf16(x):
    u = pltpu.bitcast(x.astype(BF16).astype(F32), jnp.uint32)
    return u | (u >> 16)


def _peerprep_kernel(h_ref, g2_ref, wqt_ref, keys_ref, xt_ref, cnt_ref, w1_ref, r2_ref, e2_ref):
    x = h_ref[...]
    ms = jnp.mean(x * x, axis=-1, keepdims=True)
    xn = x * lax.rsqrt(ms + EPS) * g2_ref[...]
    xt = xn.T.astype(BF16)
    xt_ref[...] = xt
    qt = jnp.dot(wqt_ref[...], xt, preferred_element_type=F32).astype(BF16)
    for h in range(PEER_HEADS):
        o = h * PEER_KEY_DIM
        s1 = jnp.dot(keys_ref[h, 0], qt[o:o + PEER_HALF], preferred_element_type=F32)
        s2 = jnp.dot(keys_ref[h, 1], qt[o + PEER_HALF:o + PEER_KEY_DIM], preferred_element_type=F32)
        v1 = _top16_desc(s1)
        v2l, rank2 = _top16_desc(s2, want_rank=True)
        v2 = _stack_rows(v2l)
        cand = jnp.concatenate([v1[0] + v2] + [v1[a] + v2[0:8] for a in range(1, PEER_TOPK)], axis=0)
        top = _top16_desc(cand)
        t0, tau = top[0], top[PEER_TOPK - 1]
        z = jnp.exp(top[0] - t0)
        for t in top[1:]:
            z = z + jnp.exp(t - t0)
        def pick(bits, lo):
            if not bits:
                return v2l[lo]
            (mask, weight), rest = bits[0], bits[1:]
            return jnp.where(mask, pick(rest, lo + weight), pick(rest, lo))

        bits = []
        for weight in (8, 4, 2, 1):
            bits.append((s1 + pick(bits, weight - 1) >= tau, weight))
        cnt = jnp.where(s1 + v2l[PEER_TOPK - 1] >= tau, 1.0, 0.0)
        for mask, weight in bits:
            cnt = cnt + jnp.where(mask, float(weight), 0.0)
        cnt = _dup_bf16(cnt)
        w1 = _dup_bf16(0.5 * jnp.exp(s1 - v1[0]) / z)
        for c in range(x.shape[0] // LANES):
            cnt_ref[h, c] = cnt[:, c * LANES:(c + 1) * LANES]
            w1_ref[h, c] = w1[:, c * LANES:(c + 1) * LANES]
        r2_ref[h] = rank2.astype(BF16)
        e2_ref[h] = jnp.exp(s2 - v2l[0]).astype(BF16)


def _peerprep_call(h, lw):
    S = h.shape[0]
    tm = min(TM_PREP, S)
    hs = pl.BlockSpec((PEER_HEADS, PEER_N_KEYS, tm), lambda i: (0, 0, i))
    hshape = jax.ShapeDtypeStruct((PEER_HEADS, PEER_N_KEYS, S), BF16)
    cs = pl.BlockSpec((PEER_HEADS, tm // LANES, PEER_N_KEYS, LANES), lambda i: (0, i, 0, 0))
    cshape = jax.ShapeDtypeStruct((PEER_HEADS, S // LANES, PEER_N_KEYS, LANES), jnp.uint32)
    return pl.pallas_call(
        _peerprep_kernel, grid=(S // tm,),
        in_specs=[pl.BlockSpec((tm, D_MODEL), lambda i: (i, 0)), _full((1, D_MODEL)),
                  _full((D_MODEL, D_MODEL)), _full((PEER_HEADS, 2, PEER_N_KEYS, PEER_HALF))],
        out_specs=[pl.BlockSpec((D_MODEL, tm), lambda i: (0, i)), cs, cs, hs, hs],
        out_shape=[jax.ShapeDtypeStruct((D_MODEL, S), BF16), cshape, cshape, hshape, hshape],
        compiler_params=_cparams(("parallel",)), name="peerprep",
    )(h, lw["g2"], lw["wqt"], lw["keys"])


def _replicated_row(ref, h, c, row):
    return ref[h, c, pl.ds(row, 8, stride=0), :]


def _peer_kernel(h_ref, xt_ref, wu_ref, wvt_ref, cnt_ref, w1_ref, r2_in, e2_in, o_ref,
                 acc_ref, a_ref, ag_ref, r2_ref, e2_ref):
    j, e = pl.program_id(0), pl.program_id(1)

    @pl.when((j == 0) & (e == 0))
    def _():
        ag_ref[...] = jnp.zeros_like(ag_ref)
        acc_ref[...] = jnp.zeros_like(acc_ref)

    @pl.when(e == 0)
    def _():
        r2_ref[...] = r2_in[...]
        e2_ref[...] = e2_in[...]

    contrib = jnp.dot(wvt_ref[...], ag_ref[...], preferred_element_type=F32)
    acc_ref[...] = jnp.where(e == 0, 0.0, acc_ref[...] + contrib)

    a_ref[...] = jnp.dot(wu_ref[...], xt_ref[...], preferred_element_type=F32)
    rt = PEER_ROW_TILE
    zero = jnp.zeros((rt, LANES), BF16)
    for il in range(ET_PEER // PEER_N_KEYS):
        for c in range(a_ref.shape[1] // LANES):
            cl = slice(c * LANES, (c + 1) * LANES)
            cnt = [pltpu.bitcast(_replicated_row(cnt_ref, h, c, il), BF16) for h in range(PEER_HEADS)]
            w1 = [pltpu.bitcast(_replicated_row(w1_ref, h, c, il), BF16) for h in range(PEER_HEADS)]
            for r in range(0, PEER_N_KEYS, rt):
                hr = lambda h: slice(h * PEER_N_KEYS + r, h * PEER_N_KEYS + r + rt)
                g = jnp.where(r2_ref[hr(0), cl] < cnt[0], e2_ref[hr(0), cl] * w1[0], zero)
                for h in range(1, PEER_HEADS):
                    g = g + jnp.where(r2_ref[hr(h), cl] < cnt[h], e2_ref[hr(h), cl] * w1[h], zero)
                row = il * PEER_N_KEYS + r
                ab = a_ref[row:row + rt, cl].astype(BF16)
                ag_ref[row:row + rt, cl] = (ab + ab * lax.erf(ab * (1.0 / math.sqrt(2.0)))) * g

    @pl.when(e == pl.num_programs(1) - 1)
    def _():
        o_ref[...] = h_ref[...] + acc_ref[...].T


def _peer_call(h, xt, cnt, w1, r2, e2, lw):
    S = h.shape[0]
    tm = min(TM_PEER, S)
    n_exp = lw["wu"].shape[0]
    rows = ET_PEER // PEER_N_KEYS
    n_tiles = n_exp // ET_PEER
    cur = lambda e: jnp.minimum(e, n_tiles - 1)
    prev = lambda e: jnp.maximum(e - 1, 0)
    tile = pl.BlockSpec((PEER_HEADS, tm // LANES, rows, LANES), lambda j, e: (0, j, cur(e), 0))
    res = pl.BlockSpec((PEER_HEADS * PEER_N_KEYS, tm), lambda j, e: (0, j))
    return pl.pallas_call(
        _peer_kernel, grid=(S // tm, n_tiles + 1),
        in_specs=[pl.BlockSpec((tm, D_MODEL), lambda j, e: (j, 0)),
                  pl.BlockSpec((D_MODEL, tm), lambda j, e: (0, j)),
                  pl.BlockSpec((ET_PEER, D_MODEL), lambda j, e: (cur(e), 0)),
                  pl.BlockSpec((None, D_MODEL, ET_PEER), lambda j, e: (prev(e), 0, 0)),
                  tile, tile, res, res],
        out_specs=pl.BlockSpec((tm, D_MODEL), lambda j, e: (j, 0)),
        out_shape=jax.ShapeDtypeStruct((S, D_MODEL), F32),
        scratch_shapes=[pltpu.VMEM((D_MODEL, tm), F32), pltpu.VMEM((ET_PEER, tm), F32),
                        pltpu.VMEM((ET_PEER, tm), BF16),
                        pltpu.VMEM((PEER_HEADS * PEER_N_KEYS, tm), BF16),
                        pltpu.VMEM((PEER_HEADS * PEER_N_KEYS, tm), BF16)],
        compiler_params=_cparams(("arbitrary", "arbitrary")), name="peer",
    )(h, xt, lw["wu"], lw["wvt"], cnt, w1, r2.reshape(-1, S), e2.reshape(-1, S))


def _pad_heads(w, n_heads, width):
    lead = w.shape[:-1]
    w = w.reshape(*lead, n_heads, width)
    w = jnp.pad(w, [(0, 0)] * len(lead) + [(0, 0), (0, LANES - width)])
    return w.reshape(*lead, n_heads * LANES)


def _dup_kv(w):
    lead = w.shape[:-1]
    w = w.reshape(*lead, A_KV_HEADS, 1, HEAD_DIM)
    w = jnp.broadcast_to(w, (*lead, A_KV_HEADS, A_HEADS // A_KV_HEADS, HEAD_DIM))
    return w.reshape(*lead, A_HEADS * HEAD_DIM)


def _block_diag(width, segments):
    m = np.zeros((width, width), np.float32)
    for a, b, d in segments:
        m[a:b, a:b] = 1.0 / d
    return jnp.asarray(m, BF16)


def _consts():
    seg64 = [(i * 64, (i + 1) * 64, 64) for i in range(4)]
    seg128 = [(i * 128, (i + 1) * 128, 64) for i in range(2)]
    segc = []
    for i in range(2):
        segc += [(i * 128, i * 128 + 64, 64), (i * 128 + 64, (i + 1) * 128, 32)]
    return {
        "bd64": _block_diag(256, seg64), "bd128": _block_diag(256, seg128), "bdc": _block_diag(256, segc),
        "bd256": _block_diag(256, [(0, 256, 256)]), "bdkv": _block_diag(128, [(0, 128, 128)]),
    }


def _layer_weights(l, p):
    w = p["w_in"][l]
    cols = [w[:, IN_OFFS[i]:IN_OFFS[i + 1]] for i in range(len(IN_SPLITS))]
    aq, ak, av, bq, bk, bv, cq, ckv, ckr, dq, dk, dv, wgate = cols
    ckr_rep = jnp.zeros((D_MODEL, C_HEADS, LANES), F32).at[:, :, C_NOPE:C_QK_DIM].set(ckr[:, None, :])
    w1 = jnp.concatenate([
        aq, _dup_kv(ak), _dup_kv(av), bq, bk, bv, cq, ckv, ckr_rep.reshape(D_MODEL, C_HEADS * LANES),
        _pad_heads(dq, D_HEADS, HEAD_DIM), _pad_heads(dk, D_KV_HEADS, HEAD_DIM),
        _pad_heads(dv, D_KV_HEADS, HEAD_DIM)], axis=1).astype(BF16)

    scale = HEAD_DIM ** -0.5
    ones = lambda n: jnp.ones((n,), F32)
    ones_col = jnp.zeros((2, LANES), F32).at[:, HEAD_DIM].set(1.0).reshape(-1)
    ck_gain = p["c_k_gain"][l]
    ckr_gain = jnp.zeros((C_HEADS, LANES), F32).at[:, C_NOPE:C_QK_DIM].set(ck_gain[None, C_NOPE:])
    gains = jnp.concatenate([
        jnp.tile(p["a_q_gain"][l], 4) * scale, jnp.tile(p["a_k_gain"][l], 4), ones(256),
        jnp.tile(p["b_q_gain"][l], (1, 4)).reshape(-1) * scale, jnp.tile(p["b_k_gain"][l], (1, 4)).reshape(-1),
        ones(768),
        p["c_q_lat_gain"][l], p["c_kv_lat_gain"][l], ckr_gain.reshape(-1),
        jnp.tile(_pad_heads(p["d_q_gain"][l], 1, HEAD_DIM), 4) * (scale * LOG2E),
        jnp.tile(_pad_heads(p["d_k_gain"][l], 1, HEAD_DIM), 2), ones_col])[None, :]

    cq_gain = jnp.tile(_pad_heads(p["c_q_gain"][l], 1, C_QK_DIM), C_HEADS) * (C_QK_DIM ** -0.5 * LOG2E)
    ckn_gain = jnp.tile(_pad_heads(ck_gain[:C_NOPE], 1, C_NOPE), C_HEADS)
    wukv = p["c_w_ukv"][l].reshape(C_KV_RANK, C_HEADS, C_NOPE + C_V)
    wukv = jnp.concatenate([_pad_heads(wukv[:, :, :C_NOPE].reshape(C_KV_RANK, -1), C_HEADS, C_NOPE),
                            _pad_heads(wukv[:, :, C_NOPE:].reshape(C_KV_RANK, -1), C_HEADS, C_V)], axis=1)

    wb = p["w_branch"][l]
    pad_rows = lambda m: _pad_heads(m.T, 4, HEAD_DIM).T
    return {
        "g1": p["norm1_gain"][l][None, :], "w1": w1, "gains": gains,
        "wuq": _pad_heads(p["c_w_uq"][l], C_HEADS, C_QK_DIM).astype(BF16), "wukv": wukv.astype(BF16),
        "gcq": cq_gain[None, :], "gckn": ckn_gain[None, :],
        "wg": wgate.astype(BF16), "gb": p["gate_bias"][l][None, :],
        "wba": wb[0].astype(BF16), "wbb": wb[1].astype(BF16),
        "wbc": pad_rows(wb[2]).astype(BF16), "wbd": pad_rows(wb[3]).astype(BF16),
        "wo": p["w_out"][l].astype(BF16),
        "g2": p["norm2_gain"][l][None, :], "wqt": p["peer_w_q"][l].T.astype(BF16),
        "keys": p["peer_sub_keys"][l].astype(BF16),
        "wu": p["peer_w_u"][l].astype(BF16),
        "wvt": p["peer_w_v"][l].reshape(-1, ET_PEER, D_MODEL).transpose(0, 2, 1).astype(BF16),
    }


def _rope_tables(S):
    half = 16
    freqs = ROPE_THETA ** (-jnp.arange(half, dtype=F32) / half)

    def cs(pos):
        ang = pos.astype(F32)[:, None] * freqs[None, :]
        return jnp.cos(ang), jnp.sin(ang)

    def table(parts):
        cos, sa, sb = [], [], []
        at = 0
        for start, (c, s) in parts:
            z = jnp.zeros((S, start - at), F32)
            zh = jnp.zeros((S, half), F32)
            cos += [z + 1.0, c, c]
            sa += [z, -s, zh]
            sb += [z, zh, s]
            at = start + 2 * half
        z = jnp.zeros((S, LANES - at), F32)
        cat = lambda pieces: jnp.concatenate(pieces, axis=1)
        return jnp.stack([cat(cos + [z + 1.0]), cat(sa + [z]), cat(sb + [z])])

    pos = jnp.arange(S, dtype=jnp.int32)
    tabc = table([(C_NOPE, cs(pos))])
    n_rows = S // GRID_W
    by_row = [jnp.repeat(t, GRID_W, axis=0) for t in cs(jnp.arange(n_rows, dtype=jnp.int32))]
    by_col = [jnp.tile(t, (n_rows, 1)) for t in cs(jnp.arange(GRID_W, dtype=jnp.int32))]
    tabd = table([(0, by_row), (2 * half, by_col)])
    return tabc, tabd


def _rel_bucket(rel):
    nb = REL_BUCKETS // 2
    max_exact = nb // 2
    ret = jnp.where(rel > 0, nb, 0)
    n = jnp.abs(rel)
    nf = jnp.maximum(n, 1).astype(F32)
    large = max_exact + (jnp.log(nf / max_exact) / math.log(REL_MAX_DIST / max_exact)
                         * (nb - max_exact)).astype(jnp.int32)
    large = jnp.minimum(large, nb - 1)
    return ret + jnp.where(n < max_exact, n, large)


def _band_bias(rel_table, half, dil):
    rel = jnp.arange(3 * QBLK)[None, :] - QBLK - jnp.arange(QBLK)[:, None]
    onehot = (_rel_bucket(rel * dil)[None, :, :] == jnp.arange(REL_BUCKETS)[:, None, None]).astype(F32)
    bias = jnp.einsum("bh,bqs->hqs", rel_table.astype(F32), onehot, precision=lax.Precision.HIGHEST)
    return jnp.where((jnp.abs(rel) <= half)[None], bias, NEG_INF)


def kernel(x, rel_bias, norm1_gain, w_in, gate_bias, a_q_gain, a_k_gain, a_sink, b_q_gain, b_k_gain, c_q_lat_gain, c_kv_lat_gain, c_w_uq, c_w_ukv, c_q_gain, c_k_gain, d_q_gain, d_k_gain, w_branch, w_out, norm2_gain, peer_w_q, peer_sub_keys, peer_w_u, peer_w_v):
    p = dict(rel_bias=rel_bias, norm1_gain=norm1_gain, w_in=w_in, gate_bias=gate_bias, a_q_gain=a_q_gain,
             a_k_gain=a_k_gain, a_sink=a_sink, b_q_gain=b_q_gain, b_k_gain=b_k_gain,
             c_q_lat_gain=c_q_lat_gain, c_kv_lat_gain=c_kv_lat_gain, c_w_uq=c_w_uq, c_w_ukv=c_w_ukv,
             c_q_gain=c_q_gain, c_k_gain=c_k_gain, d_q_gain=d_q_gain, d_k_gain=d_k_gain, w_branch=w_branch,
             w_out=w_out, norm2_gain=norm2_gain, peer_w_q=peer_w_q, peer_sub_keys=peer_sub_keys,
             peer_w_u=peer_w_u, peer_w_v=peer_w_v)
    Bn, S, D = x.shape
    assert Bn == 1 and D == D_MODEL and S % (16 * QBLK) == 0
    depth = w_in.shape[0]
    consts = _consts()
    tabc, tabd = _rope_tables(S)
    bias_a = _band_bias(rel_bias[:, :A_HEADS], A_HALF_WINDOW, 1)
    bias_b = [_band_bias(rel_bias[:, A_HEADS + g * B_HEADS:A_HEADS + (g + 1) * B_HEADS], win // (2 * dil), dil)
              for g, (win, dil) in enumerate(B_PATTERNS)]

    h = x.reshape(S, D)
    for l in range(depth):
        lw = _layer_weights(l, p)
        qa, ka, va, qb, kb, vb, qc, kct, vc, qd, kdt, vd = _proj_call(h, lw, consts, tabc, tabd)
        ya, _ = _banded_call(qa, ka, va, bias_a, a_sink[l], 1)
        obs, lses = [], []
        for g, (win, dil) in enumerate(B_PATTERNS):
            view = lambda t: t[:, g * 256:(g + 1) * 256].reshape(S // dil, dil * 256)
            o, lse = _banded_call(view(qb), view(kb), view(vb), bias_b[g], None, dil)
            obs.append(o.reshape(S, 256))
            lses.append(lse.reshape(S, 256))
        yc = _flash_call(qc, kct, vc, C_HEADS, 1)
        yd = _flash_call(qd, kdt, vd, D_HEADS, D_HEADS // D_KV_HEADS)
        h = _merge_call(h, lw, ya, obs, lses, yc, yd)
        xt, cnt, w1, r2, e2 = _peerprep_call(h, lw)
        h = _peer_call(h, xt, cnt, w1, r2, e2, lw)
    return h.reshape(Bn, S, D)
```
